```python
import jax
import jax.numpy as jnp
from jax import lax
import numpy as np

D_MODEL = 1024
BATCH = 4
SEQ = 4096
DEPTH = 1

CHUNK = 64
Q_BLOCK = 128
FOX_HEAD_DIM = 64
FOX_WIDTH = D_MODEL // 2
FOX_HEADS = FOX_WIDTH // FOX_HEAD_DIM
MLSTM_HEAD_DIM = 128
MLSTM_WIDTH = D_MODEL - FOX_WIDTH
MLSTM_HEADS = MLSTM_WIDTH // MLSTM_HEAD_DIM
MIX_WIDTH = FOX_WIDTH + MLSTM_WIDTH
CONV_WIDTH = 4
D_FF = 2816
N_MOD = 9
EPS = 1e-6
IN_SIZES = (FOX_WIDTH, FOX_WIDTH, FOX_WIDTH, FOX_HEADS,
            MLSTM_WIDTH, MLSTM_WIDTH, MLSTM_WIDTH, MLSTM_WIDTH, MLSTM_HEADS, MLSTM_HEADS)
IN_COLS = sum(IN_SIZES)

kernel_name = 'hymba_fox_mlstm_macaron_adaln'


def _split_points(sizes):
    pts, acc = [], 0
    for s in sizes[:-1]:
        acc += s
        pts.append(acc)
    return pts


def rms_normalize(x):
    xf = x.astype(jnp.float32)
    y = xf * lax.rsqrt(jnp.mean(xf * xf, axis=-1, keepdims=True) + EPS)
    return y.astype(x.dtype)


def modulate(h, shift, scale):
    return h * (1 + scale[:, None, :]) + shift[:, None, :]


def swiglu(h, w_in, w_out):
    g, u = jnp.split(h @ w_in, 2, axis=-1)
    return (jax.nn.silu(g) * u) @ w_out


def to_heads(t, n_heads):
    b, s, _ = t.shape
    return t.reshape(b, s, n_heads, -1).transpose(0, 2, 1, 3)


def from_heads(t):
    b, h, s, d = t.shape
    return t.transpose(0, 2, 1, 3).reshape(b, s, h * d)


def causal_depthwise_conv(x, w, b):
    ch = x.shape[-1]
    y = lax.conv_general_dilated(
        x, w[:, None, :], window_strides=(1,), padding=[(CONV_WIDTH - 1, 0)],
        dimension_numbers=('NWC', 'WIO', 'NWC'), feature_group_count=ch)
    return y + b


def forgetting_attention(q, k, v, log_f):
    b, h, s, dh = q.shape
    nb = s // Q_BLOCK
    cum = jnp.cumsum(log_f, axis=-1)
    kpos = jnp.arange(s)
    qb = q.reshape(b, h, nb, Q_BLOCK, dh).transpose(2, 0, 1, 3, 4)
    cb = cum.reshape(b, h, nb, Q_BLOCK).transpose(2, 0, 1, 3)
    qpos = kpos.reshape(nb, Q_BLOCK)
    scale = dh ** -0.5

    def block(args):
        q_i, c_i, pos_i = args
        logits = jnp.einsum('bhqd,bhkd->bhqk', q_i, k).astype(jnp.float32) * scale
        logits = logits + c_i[..., :, None] - cum[..., None, :]
        logits = jnp.where(pos_i[:, None] >= kpos[None, :], logits, -jnp.inf)
        p = jax.nn.softmax(logits, axis=-1)
        return jnp.einsum('bhqk,bhkd->bhqd', p.astype(v.dtype), v)

    out = lax.map(block, (qb, cb, qpos))
    return out.transpose(1, 2, 0, 3, 4).reshape(b, h, s, dh)


def mlstm_chunkwise(q, k, v, i_pre, log_f):
    b, h, s, dh = q.shape
    nc = s // CHUNK
    f32 = jnp.float32
    qc = q.astype(f32).reshape(b, h, nc, CHUNK, dh)
    kc = (k.astype(f32) * dh ** -0.5).reshape(b, h, nc, CHUNK, dh)
    vc = v.astype(f32).reshape(b, h, nc, CHUNK, dh)
    ic = i_pre.reshape(b, h, nc, CHUNK)
    a = jnp.cumsum(log_f.reshape(b, h, nc, CHUNK), axis=-1)
    g = a[..., -1]
    w_end = g[..., None] - a + ic

    def step(carry, xs):
        c_mat, n_vec, m = carry
        k_j, v_j, w_j, g_j = xs
        m_new = jnp.maximum(g_j + m, jnp.max(w_j, axis=-1))
        decay = jnp.exp(g_j + m - m_new)
        wt = jnp.exp(w_j - m_new[..., None])
        c_new = decay[..., None, None] * c_mat + jnp.einsum('bhl,bhld,bhle->bhde', wt, v_j, k_j)
        n_new = decay[..., None] * n_vec + jnp.einsum('bhl,bhle->bhe', wt, k_j)
        return (c_new, n_new, m_new), (c_mat, n_vec, m)

    init = (jnp.zeros((b, h, dh, dh), f32), jnp.zeros((b, h, dh), f32), jnp.zeros((b, h), f32))
    xs = (kc.transpose(2, 0, 1, 3, 4), vc.transpose(2, 0, 1, 3, 4),
          w_end.transpose(2, 0, 1, 3), g.transpose(2, 0, 1))
    _, (c_st, n_st, m_st) = lax.scan(step, init, xs)
    c_st = c_st.transpose(1, 2, 0, 3, 4)
    n_st = n_st.transpose(1, 2, 0, 3)
    m_st = m_st.transpose(1, 2, 0)

    causal = jnp.tril(jnp.ones((CHUNK, CHUNK), dtype=bool))
    d_log = a[..., :, None] - a[..., None, :] + ic[..., None, :]
    d_log = jnp.where(causal, d_log, -jnp.inf)
    inter = a + m_st[..., None]
    m_t = jnp.maximum(inter, jnp.max(d_log, axis=-1))
    p = jnp.exp(d_log - m_t[..., None])
    inter_w = jnp.exp(inter - m_t)
    scores = jnp.einsum('bhntd,bhnsd->bhnts', qc, kc) * p
    num = (jnp.einsum('bhnts,bhnsd->bhntd', scores, vc)
           + inter_w[..., None] * jnp.einsum('bhnde,bhnte->bhntd', c_st, qc))
    den = scores.sum(-1) + inter_w * jnp.einsum('bhne,bhnte->bhnt', n_st, qc)
    out = num / jnp.maximum(jnp.abs(den), jnp.exp(-m_t))[..., None]
    return out.reshape(b, h, s, dh)


def hybrid_mixer(h, w_in, b_fox_f, b_m_i, b_m_f, w_conv, b_conv, g_mhn, w_out):
    b, s, _ = h.shape
    f32 = jnp.float32
    proj = h @ w_in
    fq, fk, fv, ff, mq, mk, mv, mo, mi, mf = jnp.split(proj, _split_points(IN_SIZES), axis=-1)
    log_f_a = jax.nn.log_sigmoid((ff + b_fox_f).astype(f32)).transpose(0, 2, 1)
    y_a = from_heads(forgetting_attention(to_heads(fq, FOX_HEADS), to_heads(fk, FOX_HEADS),
                                          to_heads(fv, FOX_HEADS), log_f_a))
    qk = jax.nn.silu(causal_depthwise_conv(jnp.concatenate([mq, mk], axis=-1), w_conv, b_conv))
    mq_c, mk_c = jnp.split(qk, 2, axis=-1)
    i_pre = (mi + b_m_i).astype(f32).transpose(0, 2, 1)
    log_f_b = jax.nn.log_sigmoid((mf + b_m_f).astype(f32)).transpose(0, 2, 1)
    h_b = mlstm_chunkwise(to_heads(mq_c, MLSTM_HEADS), to_heads(mk_c, MLSTM_HEADS),
                          to_heads(mv, MLSTM_HEADS), i_pre, log_f_b)
    h_b = rms_normalize(h_b.transpose(0, 2, 1, 3)) * g_mhn.reshape(MLSTM_HEADS, MLSTM_HEAD_DIM).astype(f32)
    y_b = jax.nn.sigmoid(mo) * h_b.reshape(b, s, MLSTM_WIDTH).astype(h.dtype)
    return jnp.concatenate([y_a, y_b], axis=-1) @ w_out


def setup_inputs(seed: int = 0) -> dict:
    key = jax.random.key(seed)
    ks = jax.random.split(key, 20)
    f32 = jnp.float32

    def dense(k, shape, fan_in):
        return jax.random.normal(k, shape, f32) * fan_in ** -0.5

    def small(k, shape, s=0.02):
        return jax.random.normal(k, shape, f32) * s

    L = DEPTH
    return {
        'x': jax.random.normal(ks[0], (BATCH, SEQ, D_MODEL), f32),
        'c': jax.random.normal(ks[1], (BATCH, D_MODEL), f32),
        'w_ada': dense(ks[2], (L, D_MODEL, N_MOD * D_MODEL), D_MODEL),
        'b_ada': small(ks[3], (L, N_MOD * D_MODEL)),
        'w_ffn1_in': dense(ks[4], (L, D_MODEL, 2 * D_FF), D_MODEL),
        'w_ffn1_out': dense(ks[5], (L, D_FF, D_MODEL), D_FF),
        'w_mix_in': dense(ks[6], (L, D_MODEL, IN_COLS), D_MODEL),
        'b_fox_f': jnp.linspace(1.0, 5.0, FOX_HEADS, dtype=f32)[None] + small(ks[7], (L, FOX_HEADS), 0.1),
        'b_m_i': small(ks[8], (L, MLSTM_HEADS), 0.1),
        'b_m_f': jnp.linspace(3.0, 6.0, MLSTM_HEADS, dtype=f32)[None] + small(ks[9], (L, MLSTM_HEADS), 0.1),
        'w_conv': dense(ks[10], (L, CONV_WIDTH, 2 * MLSTM_WIDTH), CONV_WIDTH),
        'b_conv': small(ks[11], (L, 2 * MLSTM_WIDTH)),
        'g_mhn': 1.0 + small(ks[12], (L, MLSTM_WIDTH)),
        'w_mix_out': dense(ks[13], (L, MIX_WIDTH, D_MODEL), MIX_WIDTH),
        'w_ffn2_in': dense(ks[14], (L, D_MODEL, 2 * D_FF), D_MODEL),
        'w_ffn2_out': dense(ks[15], (L, D_FF, D_MODEL), D_FF),
        'g_final': 1.0 + small(ks[16], (D_MODEL,)),
    }


def reference(x, c, w_ada, b_ada, w_ffn1_in, w_ffn1_out, w_mix_in, b_fox_f, b_m_i, b_m_f,
              w_conv, b_conv, g_mhn, w_mix_out, w_ffn2_in, w_ffn2_out, g_final):
    c_act = jax.nn.silu(c)
    for layer in range(DEPTH):
        mod = c_act @ w_ada[layer] + b_ada[layer]
        (sh1, sc1, gt1, sh2, sc2, gt2, sh3, sc3, gt3) = jnp.split(mod, N_MOD, axis=-1)
        h = modulate(rms_normalize(x), sh1, sc1)
        x = x + 0.5 * gt1[:, None, :] * swiglu(h, w_ffn1_in[layer], w_ffn1_out[layer])
        h = modulate(rms_normalize(x), sh2, sc2)
        y = hybrid_mixer(h, w_mix_in[layer], b_fox_f[layer], b_m_i[layer], b_m_f[layer],
                         w_conv[layer], b_conv[layer], g_mhn[layer], w_mix_out[layer])
        x = x + gt2[:, None, :] * y
        h = modulate(rms_normalize(x), sh3, sc3)
        x = x + 0.5 * gt3[:, None, :] * swiglu(h, w_ffn2_in[layer], w_ffn2_out[layer])
    return rms_normalize(x) * g_final
```

```python
import functools

import jax
import jax.numpy as jnp
from jax import lax
from jax.experimental import pallas as pl
from jax.experimental.pallas import tpu as pltpu

F32 = jnp.float32
BF16 = jnp.bfloat16

D_MODEL = 1024
D_FF = 2816
N_MOD = 9
EPS = 1e-6
FOX_HEADS = 8
FOX_HEAD_DIM = 64
FOX_WIDTH = FOX_HEADS * FOX_HEAD_DIM
MLSTM_HEADS = 4
MLSTM_HEAD_DIM = 128
MLSTM_WIDTH = MLSTM_HEADS * MLSTM_HEAD_DIM
CONV_WIDTH = 4

LANES = 128
FOX_GROUP = 128
GATE_LANE = FOX_HEAD_DIM
MLSTM_GATE_LANE = 24
MLSTM_STATE_ROWS = MLSTM_HEAD_DIM + 16

ROW_TILE = 256
FFN_CHUNKS = (512, 512, 512, 512, 512, 256)
MLSTM_CHUNK = 256
MLSTM_SEQ_TILE = 512
FOX_Q_TILE = 512
FOX_K_TILE = 512
VMEM_LIMIT = 56 * 1024 * 1024


def _rms(x):
    return x * lax.rsqrt(jnp.mean(x * x, axis=-1, keepdims=True) + EPS)


def _split3(x):
    hi = x.astype(BF16)
    r1 = x - hi.astype(F32)
    mid = r1.astype(BF16)
    lo = (r1 - mid.astype(F32)).astype(BF16)
    return hi, mid, lo


def _ffn(h, w_in_ref, w_out_ref):
    acc = None
    j0 = 0
    for width in FFN_CHUNKS:
        g = jnp.dot(h, w_in_ref[:, j0:j0 + width], preferred_element_type=F32)
        u = jnp.dot(h, w_in_ref[:, D_FF + j0:D_FF + j0 + width], preferred_element_type=F32)
        a = (jax.nn.silu(g) * u).astype(BF16)
        part = jnp.dot(a, w_out_ref[j0:j0 + width, :], preferred_element_type=F32)
        acc = part if acc is None else acc + part
        j0 += width
    return acc


def _resident(shape):
    zeros = (0,) * len(shape)
    return pl.BlockSpec(shape, lambda *_: zeros, pipeline_mode=pl.Buffered(1))


def _ada_kernel(c_ref, w_ref, b_ref, o_ref):
    c_act = jax.nn.silu(c_ref[...])
    o_ref[...] = jnp.dot(c_act, w_ref[...], preferred_element_type=F32,
                         precision=lax.Precision.HIGHEST) + b_ref[...]


def _ada(c, w, b):
    bsz = c.shape[0]
    rows = 8
    c_pad = jnp.zeros((rows, D_MODEL), F32).at[:bsz].set(c)
    n = w.shape[1]
    tn = 1536
    out = pl.pallas_call(
        _ada_kernel,
        grid=(n // tn,),
        in_specs=[pl.BlockSpec((rows, D_MODEL), lambda j: (0, 0)),
                  pl.BlockSpec((D_MODEL, tn), lambda j: (0, j)),
                  pl.BlockSpec((1, tn), lambda j: (0, j))],
        out_specs=pl.BlockSpec((rows, tn), lambda j: (0, j)),
        out_shape=jax.ShapeDtypeStruct((rows, n), F32),
        compiler_params=pltpu.CompilerParams(dimension_semantics=("arbitrary",),
                                             vmem_limit_bytes=VMEM_LIMIT),
        name="ada",
    )(c_pad, w, b.reshape(1, n))
    return out[:bsz]


def _pre_kernel(x_ref, mod_ref, w1i_ref, w1o_ref, wq_ref, wk_ref, wv_ref, wmqk_ref, wmv_ref, wmo_ref,
                wg_ref, gbias_ref, pq_ref, pk_ref, cq_ref, ck_ref, convw_ref, convb_ref, kscale_ref,
                x1_ref, qa_ref, ka_ref, fvt_ref, mqk_ref, mvt_ref, mo_ref, gtok_ref, gt_ref,
                cum_carry, convbuf, *, tm, lm):
    j = pl.program_id(1)
    x = x_ref[0]
    sh1, sc1, gt1 = mod_ref[0, 0:1, :], mod_ref[0, 1:2, :], mod_ref[0, 2:3, :]
    sh2, sc2 = mod_ref[0, 3:4, :], mod_ref[0, 4:5, :]

    h = (_rms(x) * (1 + sc1) + sh1).astype(BF16)
    x1 = x + (0.5 * gt1) * _ffn(h, w1i_ref, w1o_ref)
    x1_ref[0] = x1
    h2 = (_rms(x1) * (1 + sc2) + sh2).astype(BF16)

    z = jnp.dot(h2, wg_ref[...], preferred_element_type=F32) + gbias_ref[...]
    za, zb = z[:, :LANES], z[:, LANES:]
    ls_a = jax.nn.log_sigmoid(za)
    ls_b = jax.nn.log_sigmoid(zb)

    row = lax.broadcasted_iota(jnp.int32, (tm, tm), 0)
    col = lax.broadcasted_iota(jnp.int32, (tm, tm), 1)
    dist = row - col
    tri_seq = jnp.where(dist >= 0, 1.0, 0.0).astype(BF16)
    tri_chunk = jnp.where(dist >= 0, jnp.where(dist <= (row & (lm - 1)), 1.0, 0.0), 0.0).astype(BF16)

    def cumsum_rows(tri, v):
        hi, mid, lo = _split3(v)
        r = jnp.dot(tri, jnp.concatenate([hi, mid, lo], axis=1), preferred_element_type=F32)
        return (r[:, :LANES] + r[:, LANES:2 * LANES]) + r[:, 2 * LANES:]

    @pl.when(j == 0)
    def _():
        cum_carry[...] = jnp.zeros_like(cum_carry)

    cum_f = cumsum_rows(tri_seq, ls_a) + cum_carry[0:1, :]
    cum_carry[0:1, :] = cum_f[tm - 1:tm, :]
    a_loc = cumsum_rows(tri_chunk, ls_b)
    u_gate = za - a_loc

    lane = lax.broadcasted_iota(jnp.int32, (tm, LANES), 1)
    c_hi = cum_f.astype(BF16).astype(F32)
    c_mid = (cum_f - c_hi).astype(BF16).astype(F32)
    c_lo = (cum_f - c_hi) - c_mid
    packed = jnp.where(lane < 8, c_hi, jnp.where(lane < 16, c_mid, jnp.where(lane < 24, c_lo, 0.0))).astype(BF16)

    q = jnp.dot(h2, wq_ref[...], preferred_element_type=F32)
    q = q + jnp.dot(packed, pq_ref[...], preferred_element_type=F32) + cq_ref[...]
    qa_ref[0] = q.astype(BF16)
    k = jnp.dot(h2, wk_ref[...], preferred_element_type=F32)
    k = k + jnp.dot(packed, pk_ref[...], preferred_element_type=F32) + ck_ref[...]
    ka_ref[0] = k.astype(BF16)

    fvt_ref[0] = jnp.dot(h2, wv_ref[...], preferred_element_type=F32).T.astype(BF16)
    mvt_ref[0] = jnp.dot(h2, wmv_ref[...], preferred_element_type=F32).T.astype(BF16)
    mo_ref[0] = jnp.dot(h2, wmo_ref[...], preferred_element_type=F32)

    mqk = jnp.dot(h2, wmqk_ref[...], preferred_element_type=F32)

    @pl.when(j == 0)
    def _():
        convbuf[0:8, :] = jnp.zeros((8, 2 * MLSTM_WIDTH), F32)

    @pl.when(j > 0)
    def _():
        convbuf[0:8, :] = convbuf[tm:tm + 8, :]

    convbuf[8:tm + 8, :] = mqk
    w = convw_ref[...]
    y = convb_ref[...] + w[3:4, :] * mqk
    y = y + w[2:3, :] * convbuf[7:tm + 7, :]
    y = y + w[1:2, :] * convbuf[6:tm + 6, :]
    y = y + w[0:1, :] * convbuf[5:tm + 5, :]
    mqk_ref[0] = (jax.nn.silu(y) * kscale_ref[...]).astype(BF16)

    gtok_ref[0] = u_gate
    gt_ref[0, 0:8, :] = a_loc.T[MLSTM_GATE_LANE:MLSTM_GATE_LANE + 8, :]
    gt_ref[0, 8:16, :] = u_gate.T[MLSTM_GATE_LANE:MLSTM_GATE_LANE + 8, :]


def _pre(x, mod, w1i, w1o, wq, wk, wv, wmqk, wmv, wmo, wg, gbias, pq, pk, cq, ck, convw, convb, kscale):
    bsz, seq, _ = x.shape
    tm = ROW_TILE
    grid = (bsz, seq // tm)
    tok = lambda width: pl.BlockSpec((1, tm, width), lambda b, j: (b, j, 0))
    feat = lambda rows: pl.BlockSpec((1, rows, tm), lambda b, j: (b, 0, j))
    consts = (w1i, w1o, wq, wk, wv, wmqk, wmv, wmo, wg, gbias, pq, pk, cq, ck, convw, convb, kscale)
    out_shape = (
        jax.ShapeDtypeStruct((bsz, seq, D_MODEL), F32),
        jax.ShapeDtypeStruct((bsz, seq, FOX_HEADS * FOX_GROUP), BF16),
        jax.ShapeDtypeStruct((bsz, seq, FOX_HEADS * FOX_GROUP), BF16),
        jax.ShapeDtypeStruct((bsz, FOX_WIDTH, seq), BF16),
        jax.ShapeDtypeStruct((bsz, seq, 2 * MLSTM_WIDTH), BF16),
        jax.ShapeDtypeStruct((bsz, MLSTM_WIDTH, seq), BF16),
        jax.ShapeDtypeStruct((bsz, seq, MLSTM_WIDTH), F32),
        jax.ShapeDtypeStruct((bsz, seq, LANES), F32),
        jax.ShapeDtypeStruct((bsz, 16, seq), F32),
    )
    out_specs = (tok(D_MODEL), tok(FOX_HEADS * FOX_GROUP), tok(FOX_HEADS * FOX_GROUP), feat(FOX_WIDTH),
                 tok(2 * MLSTM_WIDTH), feat(MLSTM_WIDTH), tok(MLSTM_WIDTH), tok(LANES), feat(16))
    return pl.pallas_call(
        functools.partial(_pre_kernel, tm=tm, lm=MLSTM_CHUNK),
        grid=grid,
        in_specs=[tok(D_MODEL), pl.BlockSpec((1, N_MOD, D_MODEL), lambda b, j: (b, 0, 0))]
        + [_resident(a.shape) for a in consts],
        out_specs=out_specs,
        out_shape=out_shape,
        scratch_shapes=[pltpu.VMEM((8, LANES), F32), pltpu.VMEM((tm + 8, 2 * MLSTM_WIDTH), F32)],
        compiler_params=pltpu.CompilerParams(dimension_semantics=("arbitrary", "arbitrary"),
                                             vmem_limit_bytes=VMEM_LIMIT),
        name="pre",
    )(x, mod, *consts)


def _fox_kernel(q_ref, k_ref, vt_ref, o_ref, *, tq, tk):
    qi = pl.program_id(2)
    outs = []
    for hh in range(2):
        lanes = slice(FOX_GROUP * hh, FOX_GROUP * (hh + 1))
        rows = slice(FOX_HEAD_DIM * hh, FOX_HEAD_DIM * (hh + 1))
        q = q_ref[0, :, lanes]

        def tile(k0, masked, carry, q=q, lanes=lanes, rows=rows):
            m, acc = carry
            kt = k_ref[0, pl.ds(k0, tk), lanes]
            s = lax.dot_general(kt, q, (((1,), (1,)), ((), ())), preferred_element_type=F32)
            if masked:
                kpos = k0 + lax.broadcasted_iota(jnp.int32, (tk, tq), 0)
                qpos = qi * tq + lax.broadcasted_iota(jnp.int32, (tk, tq), 1)
                s = jnp.where(kpos <= qpos, s, -jnp.inf)
            m_new = jnp.maximum(m, jnp.max(s, axis=0, keepdims=True))
            alpha = jnp.exp(m - m_new)
            p = jnp.exp(s - m_new).astype(BF16)
            v_aug = jnp.concatenate([vt_ref[0, rows, pl.ds(k0, tk)], jnp.ones((16, tk), BF16)], axis=0)
            acc = alpha * acc + jnp.dot(v_aug, p, preferred_element_type=F32)
            return m_new, acc

        carry = (jnp.full((1, tq), -jnp.inf, F32), jnp.zeros((FOX_HEAD_DIM + 16, tq), F32))
        carry = lax.fori_loop(0, qi * (tq // tk),
                              lambda t, c: tile(pl.multiple_of(t * tk, tk), False, c), carry)
        for r in range(tq // tk):
            carry = tile(pl.multiple_of(qi * tq + r * tk, tk), True, carry)
        _, acc = carry
        outs.append(acc[0:FOX_HEAD_DIM] / acc[FOX_HEAD_DIM:FOX_HEAD_DIM + 1])
    o_ref[0] = jnp.concatenate(outs, axis=0).T.astype(BF16)


def _fox(qa, ka, fvt):
    bsz, seq, _ = qa.shape
    tq, tk = FOX_Q_TILE, FOX_K_TILE
    return pl.pallas_call(
        functools.partial(_fox_kernel, tq=tq, tk=tk),
        grid=(bsz, FOX_HEADS // 2, seq // tq),
        in_specs=[pl.BlockSpec((1, tq, 2 * FOX_GROUP), lambda b, g, i: (b, i, g)),
                  pl.BlockSpec((1, seq, 2 * FOX_GROUP), lambda b, g, i: (b, 0, g)),
                  pl.BlockSpec((1, 2 * FOX_HEAD_DIM, seq), lambda b, g, i: (b, g, 0))],
        out_specs=pl.BlockSpec((1, tq, 2 * FOX_HEAD_DIM), lambda b, g, i: (b, i, g)),
        out_shape=jax.ShapeDtypeStruct((bsz, seq, FOX_WIDTH), BF16),
        compiler_params=pltpu.CompilerParams(dimension_semantics=("arbitrary",) * 3,
                                             vmem_limit_bytes=VMEM_LIMIT),
        name="fox",
    )(qa, ka, fvt)


def _mlstm_kernel(q_ref, k_ref, vt_ref, mo_ref, gtok_ref, gt_ref, gain_ref, o_ref, c_scr, m_scr, *, ts, lm):
    @pl.when(pl.program_id(1) == 0)
    def _():
        c_scr[...] = jnp.zeros_like(c_scr)
        m_scr[...] = jnp.zeros_like(m_scr)

    s_idx = lax.broadcasted_iota(jnp.int32, (lm, lm), 0)
    t_idx = lax.broadcasted_iota(jnp.int32, (lm, lm), 1)
    causal = s_idx <= t_idx
    dh = MLSTM_HEAD_DIM
    for c in range(ts // lm):
        tok = slice(c * lm, (c + 1) * lm)
        for h in range(MLSTM_HEADS):
            cols = slice(dh * h, dh * (h + 1))
            q = q_ref[0, tok, cols]
            k = k_ref[0, tok, cols]
            vt = vt_ref[0, cols, tok]
            a_row = gt_ref[0, h:h + 1, tok]
            u_row = gt_ref[0, 8 + h:9 + h, tok]
            u_col = gtok_ref[0, tok, MLSTM_GATE_LANE + h:MLSTM_GATE_LANE + h + 1]
            m_prev = m_scr[h:h + 1, 0:1]
            c_prev = c_scr[h]

            g = a_row[:, lm - 1:lm]
            w_end = g + u_row
            m_new = jnp.maximum(g + m_prev, jnp.max(w_end, axis=1, keepdims=True))
            decay = jnp.exp(g + m_prev - m_new)
            wt = jnp.exp(w_end - m_new)

            d_log = jnp.where(causal, a_row + u_col, -jnp.inf)
            inter = a_row + m_prev
            m_t = jnp.maximum(inter, jnp.max(d_log, axis=0, keepdims=True))
            p = jnp.exp(d_log - m_t)
            inter_w = jnp.exp(inter - m_t)

            scores = lax.dot_general(k, q, (((1,), (1,)), ((), ())), preferred_element_type=F32) * p
            cq = lax.dot_general(c_prev.astype(BF16), q, (((1,), (1,)), ((), ())),
                                 preferred_element_type=F32)
            num = jnp.dot(vt, scores.astype(BF16), preferred_element_type=F32) + inter_w * cq[0:dh]
            den = jnp.sum(scores, axis=0, keepdims=True) + inter_w * cq[dh:dh + 1]
            out_t = num / jnp.maximum(jnp.abs(den), jnp.exp(-m_t))
            hn = out_t * lax.rsqrt(jnp.mean(out_t * out_t, axis=0, keepdims=True) + EPS)
            hb = hn.T * gain_ref[:, cols]
            o_ref[0, tok, cols] = (jax.nn.sigmoid(mo_ref[0, tok, cols]) * hb).astype(BF16)

            v_w = jnp.concatenate([vt.astype(F32) * wt, jnp.broadcast_to(wt, (16, lm))], axis=0).astype(BF16)
            c_scr[h] = decay * c_prev + jnp.dot(v_w, k, preferred_element_type=F32)
            m_scr[h:h + 1, :] = jnp.broadcast_to(m_new, (1, LANES))


def _mlstm(mqk, mvt, mo, gtok, gt, gain):
    bsz, seq, _ = mo.shape
    ts, lm = MLSTM_SEQ_TILE, MLSTM_CHUNK
    return pl.pallas_call(
        functools.partial(_mlstm_kernel, ts=ts, lm=lm),
        grid=(bsz, seq // ts),
        in_specs=[pl.BlockSpec((1, ts, MLSTM_WIDTH), lambda b, i: (b, i, 0)),
                  pl.BlockSpec((1, ts, MLSTM_WIDTH), lambda b, i: (b, i, 1)),
                  pl.BlockSpec((1, MLSTM_WIDTH, ts), lambda b, i: (b, 0, i)),
                  pl.BlockSpec((1, ts, MLSTM_WIDTH), lambda b, i: (b, i, 0)),
                  pl.BlockSpec((1, ts, LANES), lambda b, i: (b, i, 0)),
                  pl.BlockSpec((1, 16, ts), lambda b, i: (b, 0, i)),
                  pl.BlockSpec((1, MLSTM_WIDTH), lambda b, i: (0, 0))],
        out_specs=pl.BlockSpec((1, ts, MLSTM_WIDTH), lambda b, i: (b, i, 0)),
        out_shape=jax.ShapeDtypeStruct((bsz, seq, MLSTM_WIDTH), BF16),
        scratch_shapes=[pltpu.VMEM((MLSTM_HEADS, MLSTM_STATE_ROWS, MLSTM_HEAD_DIM), F32),
                        pltpu.VMEM((8, LANES), F32)],
        compiler_params=pltpu.CompilerParams(dimension_semantics=("arbitrary", "arbitrary"),
                                             vmem_limit_bytes=VMEM_LIMIT),
        name="mlstm",
    )(mqk, mqk, mvt, mo, gtok, gt, gain)


def _post_kernel(x1_ref, ya_ref, yb_ref, mod_ref, woa_ref, wob_ref, w2i_ref, w2o_ref, gf_ref, o_ref):
    gt2, sh3 = mod_ref[0, 5:6, :], mod_ref[0, 6:7, :]
    sc3, gt3 = mod_ref[0, 7:8, :], mod_ref[0, 8:9, :]
    y = jnp.dot(ya_ref[0], woa_ref[...], preferred_element_type=F32)
    y = y + jnp.dot(yb_ref[0], wob_ref[...], preferred_element_type=F32)
    x2 = x1_ref[0] + gt2 * y
    h = (_rms(x2) * (1 + sc3) + sh3).astype(BF16)
    x3 = x2 + (0.5 * gt3) * _ffn(h, w2i_ref, w2o_ref)
    o_ref[0] = _rms(x3) * gf_ref[...]


def _post(x1, ya, yb, mod, woa, wob, w2i, w2o, gf):
    bsz, seq, _ = x1.shape
    tm = ROW_TILE
    tok = lambda width: pl.BlockSpec((1, tm, width), lambda b, j: (b, j, 0))
    consts = (woa, wob, w2i, w2o, gf)
    return pl.pallas_call(
        _post_kernel,
        grid=(bsz, seq // tm),
        in_specs=[tok(D_MODEL), tok(FOX_WIDTH), tok(MLSTM_WIDTH),
                  pl.BlockSpec((1, N_MOD, D_MODEL), lambda b, j: (b, 0, 0))]
        + [_resident(a.shape) for a in consts],
        out_specs=tok(D_MODEL),
        out_shape=jax.ShapeDtypeStruct((bsz, seq, D_MODEL), F32),
        compiler_params=pltpu.CompilerParams(dimension_semantics=("arbitrary", "arbitrary"),
                                             vmem_limit_bytes=VMEM_LIMIT),
        name="post",
    )(x1, ya, yb, mod, *consts)


def _fox_head_groups(w):
    w = w.reshape(D_MODEL, FOX_HEADS, FOX_HEAD_DIM)
    w = jnp.pad(w, ((0, 0), (0, 0), (0, FOX_GROUP - FOX_HEAD_DIM)))
    return w.reshape(D_MODEL, FOX_HEADS * FOX_GROUP)


def _gate_placement():
    pq = jnp.zeros((LANES, FOX_HEADS * FOX_GROUP), F32)
    pk = jnp.zeros((LANES, FOX_HEADS * FOX_GROUP), F32)
    cq = jnp.zeros((1, FOX_HEADS * FOX_GROUP), F32)
    ck = jnp.zeros((1, FOX_HEADS * FOX_GROUP), F32)
    for h in range(FOX_HEADS):
        base = FOX_GROUP * h + GATE_LANE
        for part in range(3):
            pq = pq.at[8 * part + h, base + part].set(1.0)
            pk = pk.at[8 * part + h, base + 3 + part].set(-1.0)
            cq = cq.at[0, base + 3 + part].set(1.0)
            ck = ck.at[0, base + part].set(1.0)
    return pq.astype(BF16), pk.astype(BF16), cq, ck


def kernel(x, c, w_ada, b_ada, w_ffn1_in, w_ffn1_out, w_mix_in, b_fox_f, b_m_i, b_m_f, w_conv, b_conv, g_mhn,
           w_mix_out, w_ffn2_in, w_ffn2_out, g_final):
    assert w_ada.shape[0] == 1, "one layer"
    bsz, seq, _ = x.shape
    assert seq % max(ROW_TILE, MLSTM_SEQ_TILE, FOX_Q_TILE) == 0

    mod = _ada(c, w_ada[0], b_ada[0]).reshape(bsz, N_MOD, D_MODEL)

    w_mix = w_mix_in[0]
    o = 0
    cols = {}
    for name, width in (("fq", FOX_WIDTH), ("fk", FOX_WIDTH), ("fv", FOX_WIDTH), ("ff", FOX_HEADS),
                        ("mq", MLSTM_WIDTH), ("mk", MLSTM_WIDTH), ("mv", MLSTM_WIDTH), ("mo", MLSTM_WIDTH),
                        ("mi", MLSTM_HEADS), ("mf", MLSTM_HEADS)):
        cols[name] = w_mix[:, o:o + width]
        o += width
    wq = _fox_head_groups(cols["fq"] * FOX_HEAD_DIM ** -0.5).astype(BF16)
    wk = _fox_head_groups(cols["fk"]).astype(BF16)
    wv = cols["fv"].astype(BF16)
    wmqk = jnp.concatenate([cols["mq"], cols["mk"]], axis=1).astype(BF16)
    wmv = cols["mv"].astype(BF16)
    wmo = cols["mo"].astype(BF16)
    zeros = lambda n: jnp.zeros((D_MODEL, n), F32)
    wg = jnp.concatenate([cols["ff"], cols["ff"], cols["ff"], cols["mi"], zeros(LANES - 28),
                          zeros(MLSTM_GATE_LANE), cols["mf"], zeros(LANES - 28)], axis=1).astype(BF16)
    zrow = lambda n: jnp.zeros((n,), F32)
    gbias = jnp.concatenate([b_fox_f[0], b_fox_f[0], b_fox_f[0], b_m_i[0], zrow(LANES - 28),
                             zrow(MLSTM_GATE_LANE), b_m_f[0], zrow(LANES - 28)]).reshape(1, 2 * LANES)
    pq, pk, cq, ck = _gate_placement()
    kscale = jnp.concatenate([jnp.ones((MLSTM_WIDTH,), F32),
                              jnp.full((MLSTM_WIDTH,), MLSTM_HEAD_DIM ** -0.5, F32)]).reshape(1, 2 * MLSTM_WIDTH)

    x1, qa, ka, fvt, mqk, mvt, mo, gtok, gt = _pre(
        x, mod, w_ffn1_in[0].astype(BF16), w_ffn1_out[0].astype(BF16), wq, wk, wv, wmqk, wmv, wmo, wg, gbias,
        pq, pk, cq, ck, w_conv[0], b_conv[0].reshape(1, -1), kscale)

    ya = _fox(qa, ka, fvt)
    yb = _mlstm(mqk, mvt, mo, gtok, gt, g_mhn[0].reshape(1, MLSTM_WIDTH))

    w_out = w_mix_out[0].astype(BF16)
    return _post(x1, ya, yb, mod, w_out[:FOX_WIDTH], w_out[FOX_WIDTH:], w_ffn2_in[0].astype(BF16),
                 w_ffn2_out[0].astype(BF16), g_final.reshape(1, D_MODEL))
```

```python
import functools

import jax
import jax.numpy as jnp
import numpy as np
from jax import lax
from jax.experimental import pallas as pl
from jax.experimental.pallas import tpu as pltpu

F32 = jnp.float32
BF16 = jnp.bfloat16

D_MODEL = 1024
D_FF = 2816
N_MOD = 9
EPS = 1e-6
LOG2_E = 1.4426950408889634
FOX_HEADS = 8
FOX_HEAD_DIM = 64
FOX_WIDTH = FOX_HEADS * FOX_HEAD_DIM
MLSTM_HEADS = 4
MLSTM_HEAD_DIM = 128
MLSTM_WIDTH = MLSTM_HEADS * MLSTM_HEAD_DIM
CONV_WIDTH = 4

LANES = 128
FOX_GROUP = 128
GATE_LANE = FOX_HEAD_DIM
MLSTM_GATE_LANE = 24
MLSTM_STATE_ROWS = MLSTM_HEAD_DIM + 16

ROW_TILE = 256
FFN_CHUNKS = (512, 512, 512, 512, 512, 256)
MLSTM_CHUNK = 256
MLSTM_SEQ_TILE = 512
FOX_TILE = 512
VMEM_LIMIT = 56 * 1024 * 1024


def _rms(x):
    return x * lax.rsqrt(jnp.mean(x * x, axis=-1, keepdims=True) + EPS)


def _split3(x):
    hi = x.astype(BF16)
    r1 = x - hi.astype(F32)
    mid = r1.astype(BF16)
    lo = (r1 - mid.astype(F32)).astype(BF16)
    return hi, mid, lo


def _ffn(h, w_in_ref, w_out_ref):
    acc = None
    j0 = 0
    for width in FFN_CHUNKS:
        g = jnp.dot(h, w_in_ref[:, j0:j0 + width], preferred_element_type=F32)
        u = jnp.dot(h, w_in_ref[:, D_FF + j0:D_FF + j0 + width], preferred_element_type=F32)
        a = (jax.nn.silu(g) * u).astype(BF16)
        part = jnp.dot(a, w_out_ref[j0:j0 + width, :], preferred_element_type=F32)
        acc = part if acc is None else acc + part
        j0 += width
    return acc


def _resident(shape):
    zeros = (0,) * len(shape)
    return pl.BlockSpec(shape, lambda *_: zeros, pipeline_mode=pl.Buffered(1))


def _ada_kernel(c_ref, w_ref, b_ref, o_ref):
    c_act = jax.nn.silu(c_ref[...])
    o_ref[...] = jnp.dot(c_act, w_ref[...], preferred_element_type=F32,
                         precision=lax.Precision.HIGHEST) + b_ref[...]


def _ada(c, w, b):
    bsz = c.shape[0]
    rows = 8
    c_pad = jnp.zeros((rows, D_MODEL), F32).at[:bsz].set(c)
    n = w.shape[1]
    tn = 1536
    out = pl.pallas_call(
        _ada_kernel,
        grid=(n // tn,),
        in_specs=[pl.BlockSpec((rows, D_MODEL), lambda j: (0, 0)),
                  pl.BlockSpec((D_MODEL, tn), lambda j: (0, j)),
                  pl.BlockSpec((1, tn), lambda j: (0, j))],
        out_specs=pl.BlockSpec((rows, tn), lambda j: (0, j)),
        out_shape=jax.ShapeDtypeStruct((rows, n), F32),
        compiler_params=pltpu.CompilerParams(dimension_semantics=("arbitrary",),
                                             vmem_limit_bytes=VMEM_LIMIT),
        name="ada",
    )(c_pad, w, b.reshape(1, n))
    return out[:bsz]


def _pre_kernel(x_ref, mod_ref, w1i_ref, w1o_ref, wq_ref, wk_ref, wv_ref, wmqk_ref, wmv_ref, wmo_ref,
                wg_ref, gbias_ref, pq_ref, pk_ref, cq_ref, ck_ref, convw_ref, convb_ref, kscale_ref,
                x1_ref, qa_ref, ka_ref, fvt_ref, mqk_ref, mvt_ref, mo_ref, gtok_ref, gt_ref,
                cum_carry, convbuf, *, tm, lm):
    j = pl.program_id(1)
    x = x_ref[0]
    sh1, sc1, gt1 = mod_ref[0, 0:1, :], mod_ref[0, 1:2, :], mod_ref[0, 2:3, :]
    sh2, sc2 = mod_ref[0, 3:4, :], mod_ref[0, 4:5, :]

    h = (_rms(x) * (1 + sc1) + sh1).astype(BF16)
    x1 = x + (0.5 * gt1) * _ffn(h, w1i_ref, w1o_ref)
    x1_ref[0] = x1
    h2 = (_rms(x1) * (1 + sc2) + sh2).astype(BF16)

    z = jnp.dot(h2, wg_ref[...], preferred_element_type=F32) + gbias_ref[...]
    za, zb = z[:, :LANES], z[:, LANES:]
    ls_a = jax.nn.log_sigmoid(za)
    ls_b = jax.nn.log_sigmoid(zb)

    row = lax.broadcasted_iota(jnp.int32, (tm, tm), 0)
    col = lax.broadcasted_iota(jnp.int32, (tm, tm), 1)
    dist = row - col
    tri_seq = jnp.where(dist >= 0, 1.0, 0.0).astype(BF16)
    tri_chunk = jnp.where(dist >= 0, jnp.where(dist <= (row & (lm - 1)), 1.0, 0.0), 0.0).astype(BF16)

    def cumsum_rows(tri, v):
        hi, mid, lo = _split3(v)
        r = jnp.dot(tri, jnp.concatenate([hi, mid, lo], axis=1), preferred_element_type=F32)
        return (r[:, :LANES] + r[:, LANES:2 * LANES]) + r[:, 2 * LANES:]

    @pl.when(j == 0)
    def _():
        cum_carry[...] = jnp.zeros_like(cum_carry)

    cum_f = cumsum_rows(tri_seq, ls_a) + cum_carry[0:1, :]
    cum_carry[0:1, :] = cum_f[tm - 1:tm, :]
    a_loc = cumsum_rows(tri_chunk, ls_b)
    u_gate = za - a_loc

    lane = lax.broadcasted_iota(jnp.int32, (tm, LANES), 1)
    cum_2 = cum_f * LOG2_E
    c_hi = cum_2.astype(BF16).astype(F32)
    c_mid = (cum_2 - c_hi).astype(BF16).astype(F32)
    c_lo = (cum_2 - c_hi) - c_mid
    packed = jnp.where(lane < 8, c_hi, jnp.where(lane < 16, c_mid, jnp.where(lane < 24, c_lo, 0.0))).astype(BF16)

    q = jnp.dot(h2, wq_ref[...], preferred_element_type=F32)
    q = q + jnp.dot(packed, pq_ref[...], preferred_element_type=F32) + cq_ref[...]
    qa_ref[0] = q.astype(BF16)
    k = jnp.dot(h2, wk_ref[...], preferred_element_type=F32)
    k = k + jnp.dot(packed, pk_ref[...], preferred_element_type=F32) + ck_ref[...]
    ka_ref[0] = k.astype(BF16)

    fvt_ref[0] = jnp.dot(h2, wv_ref[...], preferred_element_type=F32).T.astype(BF16)
    mvt_ref[0] = jnp.dot(h2, wmv_ref[...], preferred_element_type=F32).T.astype(BF16)
    mo_ref[0] = jnp.dot(h2, wmo_ref[...], preferred_element_type=F32)

    mqk = jnp.dot(h2, wmqk_ref[...], preferred_element_type=F32)

    @pl.when(j == 0)
    def _():
        convbuf[0:8, :] = jnp.zeros((8, 2 * MLSTM_WIDTH), F32)

    @pl.when(j > 0)
    def _():
        convbuf[0:8, :] = convbuf[tm:tm + 8, :]

    convbuf[8:tm + 8, :] = mqk
    w = convw_ref[...]
    y = convb_ref[...] + w[3:4, :] * mqk
    y = y + w[2:3, :] * convbuf[7:tm + 7, :]
    y = y + w[1:2, :] * convbuf[6:tm + 6, :]
    y = y + w[0:1, :] * convbuf[5:tm + 5, :]
    mqk_ref[0] = (jax.nn.silu(y) * kscale_ref[...]).astype(BF16)

    gtok_ref[0] = u_gate
    gt_ref[0, 0:8, :] = a_loc.T[MLSTM_GATE_LANE:MLSTM_GATE_LANE + 8, :]
    gt_ref[0, 8:16, :] = u_gate.T[MLSTM_GATE_LANE:MLSTM_GATE_LANE + 8, :]


def _pre(x, mod, w1i, w1o, wq, wk, wv, wmqk, wmv, wmo, wg, gbias, pq, pk, cq, ck, convw, convb, kscale):
    bsz, seq, _ = x.shape
    tm = ROW_TILE
    grid = (bsz, seq // tm)
    tok = lambda width: pl.BlockSpec((1, tm, width), lambda b, j: (b, j, 0))
    feat = lambda rows: pl.BlockSpec((1, rows, tm), lambda b, j: (b, 0, j))
    consts = (w1i, w1o, wq, wk, wv, wmqk, wmv, wmo, wg, gbias, pq, pk, cq, ck, convw, convb, kscale)
    out_shape = (
        jax.ShapeDtypeStruct((bsz, seq, D_MODEL), F32),
        jax.ShapeDtypeStruct((bsz, seq, FOX_HEADS * FOX_GROUP), BF16),
        jax.ShapeDtypeStruct((bsz, seq, FOX_HEADS * FOX_GROUP), BF16),
        jax.ShapeDtypeStruct((bsz, FOX_WIDTH, seq), BF16),
        jax.ShapeDtypeStruct((bsz, seq, 2 * MLSTM_WIDTH), BF16),
        jax.ShapeDtypeStruct((bsz, MLSTM_WIDTH, seq), BF16),
        jax.ShapeDtypeStruct((bsz, seq, MLSTM_WIDTH), F32),
        jax.ShapeDtypeStruct((bsz, seq, LANES), F32),
        jax.ShapeDtypeStruct((bsz, 16, seq), F32),
    )
    out_specs = (tok(D_MODEL), tok(FOX_HEADS * FOX_GROUP), tok(FOX_HEADS * FOX_GROUP), feat(FOX_WIDTH),
                 tok(2 * MLSTM_WIDTH), feat(MLSTM_WIDTH), tok(MLSTM_WIDTH), tok(LANES), feat(16))
    return pl.pallas_call(
        functools.partial(_pre_kernel, tm=tm, lm=MLSTM_CHUNK),
        grid=grid,
        in_specs=[tok(D_MODEL), pl.BlockSpec((1, N_MOD, D_MODEL), lambda b, j: (b, 0, 0))]
        + [_resident(a.shape) for a in consts],
        out_specs=out_specs,
        out_shape=out_shape,
        scratch_shapes=[pltpu.VMEM((8, LANES), F32), pltpu.VMEM((tm + 8, 2 * MLSTM_WIDTH), F32)],
        compiler_params=pltpu.CompilerParams(dimension_semantics=("arbitrary", "arbitrary"),
                                             vmem_limit_bytes=VMEM_LIMIT),
        name="pre",
    )(x, mod, *consts)


def _fox_kernel(q_ref, k_ref, vt_ref, o_ref, s_a, s_b, p_a, p_b, m_scr, alpha_scr, acc_scr, *, tq):
    qi = pl.program_id(2)
    heads = range(2)
    lanes = [slice(FOX_GROUP * h, FOX_GROUP * (h + 1)) for h in heads]
    rows = [slice(FOX_HEAD_DIM * h, FOX_HEAD_DIM * (h + 1)) for h in heads]

    def put_scores(s_buf, h, k0):
        s_buf[h] = lax.dot_general(k_ref[0, pl.ds(k0, tq), lanes[h]], q_ref[0, :, lanes[h]],
                                   (((1,), (1,)), ((), ())), preferred_element_type=F32)

    def softmax(s_buf, p_buf, h, masked):
        s = s_buf[h]
        if masked:
            kpos = lax.broadcasted_iota(jnp.int32, (tq, tq), 0)
            qpos = lax.broadcasted_iota(jnp.int32, (tq, tq), 1)
            s = jnp.where(kpos <= qpos, s, -jnp.inf)
        m = m_scr[h, 0:1, :]
        m_new = jnp.maximum(m, jnp.max(s, axis=0, keepdims=True))
        alpha_scr[h, 0:1, :] = jnp.exp2(m - m_new)
        p_buf[h] = jnp.exp2(s - m_new).astype(BF16)
        m_scr[h, 0:1, :] = m_new

    def weighted_values(p_buf, h, k0):
        v_aug = jnp.concatenate([vt_ref[0, rows[h], pl.ds(k0, tq)], jnp.ones((16, tq), BF16)], axis=0)
        acc_scr[h] = alpha_scr[h, 0:1, :] * acc_scr[h] + jnp.dot(v_aug, p_buf[h], preferred_element_type=F32)

    def step(t, s_cur, s_nxt, p_cur, p_prev, masked):
        k0 = pl.multiple_of(t * tq, tq)
        for h in heads:
            weighted_values(p_prev, h, pl.multiple_of(jnp.maximum(k0 - tq, 0), tq))
            if s_nxt is not None:
                put_scores(s_nxt, h, pl.multiple_of(k0 + tq, tq))
            softmax(s_cur, p_cur, h, masked)

    def finish(p_last):
        outs = []
        for h in heads:
            weighted_values(p_last, h, pl.multiple_of(qi * tq, tq))
            a = acc_scr[h]
            outs.append(a[0:FOX_HEAD_DIM] / a[FOX_HEAD_DIM:FOX_HEAD_DIM + 1])
        o_ref[0] = jnp.concatenate(outs, axis=0).T.astype(BF16)

    m_scr[...] = jnp.full(m_scr.shape, -jnp.inf, F32)
    alpha_scr[...] = jnp.ones(alpha_scr.shape, F32)
    acc_scr[...] = jnp.zeros(acc_scr.shape, F32)
    p_b[...] = jnp.zeros(p_b.shape, BF16)
    for h in heads:
        put_scores(s_a, h, 0)

    @pl.loop(0, qi // 2)
    def _(u):
        step(2 * u, s_a, s_b, p_a, p_b, False)
        step(2 * u + 1, s_b, s_a, p_b, p_a, False)

    @pl.when(qi % 2 == 0)
    def _():
        step(qi, s_a, None, p_a, p_b, True)
        finish(p_a)

    @pl.when(qi % 2 == 1)
    def _():
        step(qi - 1, s_a, s_b, p_a, p_b, False)
        step(qi, s_b, None, p_b, p_a, True)
        finish(p_b)


def _fox(qa, ka, fvt):
    bsz, seq, _ = qa.shape
    tq = FOX_TILE
    return pl.pallas_call(
        functools.partial(_fox_kernel, tq=tq),
        grid=(bsz, FOX_HEADS // 2, seq // tq),
        in_specs=[pl.BlockSpec((1, tq, 2 * FOX_GROUP), lambda b, g, i: (b, i, g)),
                  pl.BlockSpec((1, seq, 2 * FOX_GROUP), lambda b, g, i: (b, 0, g)),
                  pl.BlockSpec((1, 2 * FOX_HEAD_DIM, seq), lambda b, g, i: (b, g, 0))],
        out_specs=pl.BlockSpec((1, tq, 2 * FOX_HEAD_DIM), lambda b, g, i: (b, i, g)),
        out_shape=jax.ShapeDtypeStruct((bsz, seq, FOX_WIDTH), BF16),
        scratch_shapes=[pltpu.VMEM((2, tq, tq), F32), pltpu.VMEM((2, tq, tq), F32),
                        pltpu.VMEM((2, tq, tq), BF16), pltpu.VMEM((2, tq, tq), BF16),
                        pltpu.VMEM((2, 8, tq), F32), pltpu.VMEM((2, 8, tq), F32),
                        pltpu.VMEM((2, FOX_HEAD_DIM + 16, tq), F32)],
        compiler_params=pltpu.CompilerParams(dimension_semantics=("arbitrary",) * 3,
                                             vmem_limit_bytes=VMEM_LIMIT),
        name="fox",
    )(qa, ka, fvt)


def _mlstm_kernel(q_ref, k_ref, vt_ref, mo_ref, gtok_ref, gt_ref, gain_ref, o_ref, c_scr, m_scr, *, ts, lm):
    @pl.when(pl.program_id(1) == 0)
    def _():
        c_scr[...] = jnp.zeros_like(c_scr)
        m_scr[...] = jnp.zeros_like(m_scr)

    s_idx = lax.broadcasted_iota(jnp.int32, (lm, lm), 0)
    t_idx = lax.broadcasted_iota(jnp.int32, (lm, lm), 1)
    causal = s_idx <= t_idx
    dh = MLSTM_HEAD_DIM
    for c in range(ts // lm):
        tok = slice(c * lm, (c + 1) * lm)
        for h in range(MLSTM_HEADS):
            cols = slice(dh * h, dh * (h + 1))
            q = q_ref[0, tok, cols]
            k = k_ref[0, tok, cols]
            vt = vt_ref[0, cols, tok]
            a_row = gt_ref[0, h:h + 1, tok]
            u_row = gt_ref[0, 8 + h:9 + h, tok]
            u_col = gtok_ref[0, tok, MLSTM_GATE_LANE + h:MLSTM_GATE_LANE + h + 1]
            m_prev = m_scr[h:h + 1, 0:1]
            c_prev = c_scr[h]

            g = a_row[:, lm - 1:lm]
            w_end = g + u_row
            m_new = jnp.maximum(g + m_prev, jnp.max(w_end, axis=1, keepdims=True))
            decay = jnp.exp(g + m_prev - m_new)
            wt = jnp.exp(w_end - m_new)

            d_log = jnp.where(causal, a_row + u_col, -jnp.inf)
            inter = a_row + m_prev
            m_t = jnp.maximum(inter, jnp.max(d_log, axis=0, keepdims=True))
            p = jnp.exp(d_log - m_t)
            inter_w = jnp.exp(inter - m_t)

            scores = lax.dot_general(k, q, (((1,), (1,)), ((), ())), preferred_element_type=F32) * p
            cq = lax.dot_general(c_prev.astype(BF16), q, (((1,), (1,)), ((), ())),
                                 preferred_element_type=F32)
            num = jnp.dot(vt, scores.astype(BF16), preferred_element_type=F32) + inter_w * cq[0:dh]
            den = jnp.sum(scores, axis=0, keepdims=True) + inter_w * cq[dh:dh + 1]
            out_t = num / jnp.maximum(jnp.abs(den), jnp.exp(-m_t))
            hn = out_t * lax.rsqrt(jnp.mean(out_t * out_t, axis=0, keepdims=True) + EPS)
            hb = hn.T * gain_ref[:, cols]
            o_ref[0, tok, cols] = (jax.nn.sigmoid(mo_ref[0, tok, cols]) * hb).astype(BF16)

            v_w = jnp.concatenate([vt.astype(F32) * wt, jnp.broadcast_to(wt, (16, lm))], axis=0).astype(BF16)
            c_scr[h] = decay * c_prev + jnp.dot(v_w, k, preferred_element_type=F32)
            m_scr[h:h + 1, :] = jnp.broadcast_to(m_new, (1, LANES))


def _mlstm(mqk, mvt, mo, gtok, gt, gain):
    bsz, seq, _ = mo.shape
    ts, lm = MLSTM_SEQ_TILE, MLSTM_CHUNK
    return pl.pallas_call(
        functools.partial(_mlstm_kernel, ts=ts, lm=lm),
        grid=(bsz, seq // ts),
        in_specs=[pl.BlockSpec((1, ts, MLSTM_WIDTH), lambda b, i: (b, i, 0)),
                  pl.BlockSpec((1, ts, MLSTM_WIDTH), lambda b, i: (b, i, 1)),
                  pl.BlockSpec((1, MLSTM_WIDTH, ts), lambda b, i: (b, 0, i)),
                  pl.BlockSpec((1, ts, MLSTM_WIDTH), lambda b, i: (b, i, 0)),
                  pl.BlockSpec((1, ts, LANES), lambda b, i: (b, i, 0)),
                  pl.BlockSpec((1, 16, ts), lambda b, i: (b, 0, i)),
                  pl.BlockSpec((1, MLSTM_WIDTH), lambda b, i: (0, 0))],
        out_specs=pl.BlockSpec((1, ts, MLSTM_WIDTH), lambda b, i: (b, i, 0)),
        out_shape=jax.ShapeDtypeStruct((bsz, seq, MLSTM_WIDTH), BF16),
        scratch_shapes=[pltpu.VMEM((MLSTM_HEADS, MLSTM_STATE_ROWS, MLSTM_HEAD_DIM), F32),
                        pltpu.VMEM((8, LANES), F32)],
        compiler_params=pltpu.CompilerParams(dimension_semantics=("arbitrary", "arbitrary"),
                                             vmem_limit_bytes=VMEM_LIMIT),
        name="mlstm",
    )(mqk, mqk, mvt, mo, gtok, gt, gain)


def _post_kernel(x1_ref, ya_ref, yb_ref, mod_ref, woa_ref, wob_ref, w2i_ref, w2o_ref, gf_ref, o_ref):
    gt2, sh3 = mod_ref[0, 5:6, :], mod_ref[0, 6:7, :]
    sc3, gt3 = mod_ref[0, 7:8, :], mod_ref[0, 8:9, :]
    y = jnp.dot(ya_ref[0], woa_ref[...], preferred_element_type=F32)
    y = y + jnp.dot(yb_ref[0], wob_ref[...], preferred_element_type=F32)
    x2 = x1_ref[0] + gt2 * y
    h = (_rms(x2) * (1 + sc3) + sh3).astype(BF16)
    x3 = x2 + (0.5 * gt3) * _ffn(h, w2i_ref, w2o_ref)
    o_ref[0] = _rms(x3) * gf_ref[...]


def _post(x1, ya, yb, mod, woa, wob, w2i, w2o, gf):
    bsz, seq, _ = x1.shape
    tm = ROW_TILE
    tok = lambda width: pl.BlockSpec((1, tm, width), lambda b, j: (b, j, 0))
    consts = (woa, wob, w2i, w2o, gf)
    return pl.pallas_call(
        _post_kernel,
        grid=(bsz, seq // tm),
        in_specs=[tok(D_MODEL), tok(FOX_WIDTH), tok(MLSTM_WIDTH),
                  pl.BlockSpec((1, N_MOD, D_MODEL), lambda b, j: (b, 0, 0))]
        + [_resident(a.shape) for a in consts],
        out_specs=tok(D_MODEL),
        out_shape=jax.ShapeDtypeStruct((bsz, seq, D_MODEL), F32),
        compiler_params=pltpu.CompilerParams(dimension_semantics=("arbitrary", "arbitrary"),
                                             vmem_limit_bytes=VMEM_LIMIT),
        name="post",
    )(x1, ya, yb, mod, *consts)


def _fox_head_groups(w):
    w = w.reshape(D_MODEL, FOX_HEADS, FOX_HEAD_DIM)
    w = jnp.pad(w, ((0, 0), (0, 0), (0, FOX_GROUP - FOX_HEAD_DIM)))
    return w.reshape(D_MODEL, FOX_HEADS * FOX_GROUP)


def _gate_placement():
    pq = np.zeros((LANES, FOX_HEADS * FOX_GROUP), np.float32)
    pk = np.zeros((LANES, FOX_HEADS * FOX_GROUP), np.float32)
    cq = np.zeros((1, FOX_HEADS * FOX_GROUP), np.float32)
    ck = np.zeros((1, FOX_HEADS * FOX_GROUP), np.float32)
    for h in range(FOX_HEADS):
        base = FOX_GROUP * h + GATE_LANE
        for part in range(3):
            pq[8 * part + h, base + part] = 1.0
            pk[8 * part + h, base + 3 + part] = -1.0
            cq[0, base + 3 + part] = 1.0
            ck[0, base + part] = 1.0
    return jnp.asarray(pq, BF16), jnp.asarray(pk, BF16), jnp.asarray(cq), jnp.asarray(ck)


def kernel(x, c, w_ada, b_ada, w_ffn1_in, w_ffn1_out, w_mix_in, b_fox_f, b_m_i, b_m_f, w_conv, b_conv, g_mhn,
           w_mix_out, w_ffn2_in, w_ffn2_out, g_final):
    assert w_ada.shape[0] == 1, "one layer"
    bsz, seq, _ = x.shape
    assert seq % max(ROW_TILE, MLSTM_SEQ_TILE, FOX_TILE) == 0

    mod = _ada(c, w_ada[0], b_ada[0]).reshape(bsz, N_MOD, D_MODEL)

    w_mix = w_mix_in[0]
    o = 0
    cols = {}
    for name, width in (("fq", FOX_WIDTH), ("fk", FOX_WIDTH), ("fv", FOX_WIDTH), ("ff", FOX_HEADS),
                        ("mq", MLSTM_WIDTH), ("mk", MLSTM_WIDTH), ("mv", MLSTM_WIDTH), ("mo", MLSTM_WIDTH),
                        ("mi", MLSTM_HEADS), ("mf", MLSTM_HEADS)):
        cols[name] = w_mix[:, o:o + width]
        o += width
    wq = _fox_head_groups(cols["fq"] * (FOX_HEAD_DIM ** -0.5 * LOG2_E)).astype(BF16)
    wk = _fox_head_groups(cols["fk"]).astype(BF16)
    wv = cols["fv"].astype(BF16)
    wmqk = jnp.concatenate([cols["mq"], cols["mk"]], axis=1).astype(BF16)
    wmv = cols["mv"].astype(BF16)
    wmo = cols["mo"].astype(BF16)
    zeros = lambda n: jnp.zeros((D_MODEL, n), F32)
    wg = jnp.concatenate([cols["ff"], cols["ff"], cols["ff"], cols["mi"], zeros(LANES - 28),
                          zeros(MLSTM_GATE_LANE), cols["mf"], zeros(LANES - 28)], axis=1).astype(BF16)
    zrow = lambda n: jnp.zeros((n,), F32)
    gbias = jnp.concatenate([b_fox_f[0], b_fox_f[0], b_fox_f[0], b_m_i[0], zrow(LANES - 28),
                             zrow(MLSTM_GATE_LANE), b_m_f[0], zrow(LANES - 28)]).reshape(1, 2 * LANES)
    pq, pk, cq, ck = _gate_placement()
    kscale = jnp.concatenate([jnp.ones((MLSTM_WIDTH,), F32),
                              jnp.full((MLSTM_WIDTH,), MLSTM_HEAD_DIM ** -0.5, F32)]).reshape(1, 2 * MLSTM_WIDTH)

    x1, qa, ka, fvt, mqk, mvt, mo, gtok, gt = _pre(
        x, mod, w_ffn1_in[0].astype(BF16), w_ffn1_out[0].astype(BF16), wq, wk, wv, wmqk, wmv, wmo, wg, gbias,
        pq, pk, cq, ck, w_conv[0], b_conv[0].reshape(1, -1), kscale)

    ya = _fox(qa, ka, fvt)
    yb = _mlstm(mqk, mvt, mo, gtok, gt, g_mhn[0].reshape(1, MLSTM_WIDTH))

    w_out = w_mix_out[0].astype(BF16)
    return _post(x1, ya, yb, mod, w_out[:FOX_WIDTH], w_out[FOX_WIDTH:], w_ffn2_in[0].astype(BF16),
                 w_ffn2_out[0].astype(BF16), g_final.reshape(1, D_MODEL))
```

```python
import functools

import jax
import jax.numpy as jnp
import numpy as np
from jax import lax
from jax.experimental import pallas as pl
from jax.experimental.pallas import tpu as pltpu

F32 = jnp.float32
BF16 = jnp.bfloat16

D_MODEL = 1024
D_FF = 2816
N_MOD = 9
EPS = 1e-6
LOG2_E = 1.4426950408889634
FOX_HEADS = 8
FOX_HEAD_DIM = 64
FOX_WIDTH = FOX_HEADS * FOX_HEAD_DIM
MLSTM_HEADS = 4
MLSTM_HEAD_DIM = 128
MLSTM_WIDTH = MLSTM_HEADS * MLSTM_HEAD_DIM
CONV_WIDTH = 4

LANES = 128
VEC_ROWS = 16
FOX_GROUP = 128
GATE_LANE = FOX_HEAD_DIM
MLSTM_GATE_LANE = 24
MLSTM_STATE_ROWS = MLSTM_HEAD_DIM + 16

ROW_TILE = 256
FFN_CHUNKS = (512, 512, 512, 512, 512, 256)
MLSTM_CHUNK = 256
MLSTM_SEQ_TILE = 512
FOX_TILE = 512
FOX_STEP_HEADS = 4
FOX_ISSUE_HEADS = 2
FOX_ROW_BLOCK = 32
VMEM_LIMIT = 56 * 1024 * 1024


def _rms(x):
    return x * lax.rsqrt(jnp.mean(x * x, axis=-1, keepdims=True) + EPS)


def _by_rows(fn, rows, step):
    return jnp.concatenate([fn(slice(r, r + step)) for r in range(0, rows, step)], axis=0)


def _rows(row):
    return jnp.broadcast_to(row, (VEC_ROWS, row.shape[-1]))


def _split3(x):
    hi = x.astype(BF16)
    r1 = x - hi.astype(F32)
    mid = r1.astype(BF16)
    lo = (r1 - mid.astype(F32)).astype(BF16)
    return hi, mid, lo


def _ffn(h, w_in_ref, w_out_ref):
    acc = None
    j0 = 0
    for width in FFN_CHUNKS:
        g = jnp.dot(h, w_in_ref[:, j0:j0 + width], preferred_element_type=F32)
        u = jnp.dot(h, w_in_ref[:, D_FF + j0:D_FF + j0 + width], preferred_element_type=F32)
        a = _by_rows(lambda r: (jax.nn.silu(g[r]) * u[r]).astype(BF16), g.shape[0], 2 * VEC_ROWS)
        part = jnp.dot(a, w_out_ref[j0:j0 + width, :], preferred_element_type=F32)
        acc = part if acc is None else acc + part
        j0 += width
    return acc


def _resident(shape):
    zeros = (0,) * len(shape)
    return pl.BlockSpec(shape, lambda *_: zeros, pipeline_mode=pl.Buffered(1))


def _ada_kernel(c_ref, w_ref, b_ref, o_ref):
    c_act = jax.nn.silu(c_ref[...])
    o_ref[...] = jnp.dot(c_act, w_ref[...], preferred_element_type=F32,
                         precision=lax.Precision.HIGHEST) + b_ref[...]


def _ada(c, w, b):
    bsz = c.shape[0]
    rows = 8
    c_pad = jnp.zeros((rows, D_MODEL), F32).at[:bsz].set(c)
    n = w.shape[1]
    tn = 1536
    out = pl.pallas_call(
        _ada_kernel,
        grid=(n // tn,),
        in_specs=[pl.BlockSpec((rows, D_MODEL), lambda j: (0, 0)),
                  pl.BlockSpec((D_MODEL, tn), lambda j: (0, j)),
                  pl.BlockSpec((1, tn), lambda j: (0, j))],
        out_specs=pl.BlockSpec((rows, tn), lambda j: (0, j)),
        out_shape=jax.ShapeDtypeStruct((rows, n), F32),
        compiler_params=pltpu.CompilerParams(dimension_semantics=("arbitrary",),
                                             vmem_limit_bytes=VMEM_LIMIT),
        name="ada",
    )(c_pad, w, b.reshape(1, n))
    return out[:bsz]


def _pre_kernel(x_ref, mod_ref, w1i_ref, w1o_ref, wq_ref, wk_ref, wv_ref, wmqk_ref, wmv_ref, wmo_ref,
                wg_ref, gbias_ref, pq_ref, pk_ref, cq_ref, ck_ref, convw_ref, convb_ref, kscale_ref,
                x1_ref, qa_ref, ka_ref, fvt_ref, mqk_ref, mvt_ref, mo_ref, gtok_ref, gt_ref,
                cum_carry, convbuf, *, tm, lm):
    j = pl.program_id(1)
    sh1, sc1p, gt1h = _rows(mod_ref[0, 0:1, :]), _rows(1 + mod_ref[0, 1:2, :]), _rows(0.5 * mod_ref[0, 2:3, :])
    sh2, sc2p = _rows(mod_ref[0, 3:4, :]), _rows(1 + mod_ref[0, 4:5, :])

    row = lax.broadcasted_iota(jnp.int32, (tm, tm), 0)
    col = lax.broadcasted_iota(jnp.int32, (tm, tm), 1)
    dist = row - col
    tri_seq = jnp.where(dist >= 0, 1.0, 0.0).astype(BF16)
    tri_chunk = jnp.where(dist >= 0, jnp.where(dist <= (row & (lm - 1)), 1.0, 0.0), 0.0).astype(BF16)

    def cumsum_rows(tri, v):
        hi, mid, lo = _split3(v)
        r = jnp.dot(tri, jnp.concatenate([hi, mid, lo], axis=1), preferred_element_type=F32)
        return (r[:, :LANES] + r[:, LANES:2 * LANES]) + r[:, 2 * LANES:]

    h = _by_rows(lambda r: (_rms(x_ref[0, r, :]) * sc1p + sh1).astype(BF16), tm, VEC_ROWS)
    ffn1 = _ffn(h, w1i_ref, w1o_ref)

    def residual_and_norm(r):
        x1 = x_ref[0, r, :] + gt1h * ffn1[r]
        x1_ref[0, r, :] = x1
        return (_rms(x1) * sc2p + sh2).astype(BF16)

    h2 = _by_rows(residual_and_norm, tm, VEC_ROWS)

    z = jnp.dot(h2, wg_ref[...], preferred_element_type=F32) + gbias_ref[...]
    mqk = jnp.dot(h2, wmqk_ref[...], preferred_element_type=F32)

    za, zb = z[:, :LANES], z[:, LANES:]
    ls_a = jax.nn.log_sigmoid(za)
    ls_b = jax.nn.log_sigmoid(zb)

    @pl.when(j == 0)
    def _():
        cum_carry[...] = jnp.zeros_like(cum_carry)
        convbuf[0:8, :] = jnp.zeros((8, 2 * MLSTM_WIDTH), F32)

    @pl.when(j > 0)
    def _():
        convbuf[0:8, :] = convbuf[tm:tm + 8, :]

    v = jnp.dot(h2, wv_ref[...], preferred_element_type=F32)

    cum_f = cumsum_rows(tri_seq, ls_a) + cum_carry[0:1, :]
    cum_carry[0:1, :] = cum_f[tm - 1:tm, :]
    a_loc = cumsum_rows(tri_chunk, ls_b)
    u_gate = za - a_loc

    lane = lax.broadcasted_iota(jnp.int32, (tm, LANES), 1)
    cum_2 = cum_f * LOG2_E
    c_hi = cum_2.astype(BF16).astype(F32)
    c_mid = (cum_2 - c_hi).astype(BF16).astype(F32)
    c_lo = (cum_2 - c_hi) - c_mid
    packed = jnp.where(lane < 8, c_hi, jnp.where(lane < 16, c_mid, jnp.where(lane < 24, c_lo, 0.0))).astype(BF16)

    mv = jnp.dot(h2, wmv_ref[...], preferred_element_type=F32)

    convbuf[8:tm + 8, :] = mqk
    for c in range(0, 2 * MLSTM_WIDTH, MLSTM_WIDTH):
        cols = slice(c, c + MLSTM_WIDTH)
        taps = [_rows(convw_ref[t:t + 1, cols]) for t in range(CONV_WIDTH)]
        bias, out_scale = _rows(convb_ref[:, cols]), _rows(kscale_ref[:, cols])
        for r in range(0, tm, VEC_ROWS):
            y = bias + taps[3] * convbuf[8 + r:8 + r + VEC_ROWS, cols]
            y = y + taps[2] * convbuf[7 + r:7 + r + VEC_ROWS, cols]
            y = y + taps[1] * convbuf[6 + r:6 + r + VEC_ROWS, cols]
            y = y + taps[0] * convbuf[5 + r:5 + r + VEC_ROWS, cols]
            mqk_ref[0, r:r + VEC_ROWS, cols] = (jax.nn.silu(y) * out_scale).astype(BF16)

    mo_ref[0] = jnp.dot(h2, wmo_ref[...], preferred_element_type=F32)
    fvt_ref[0] = v.T.astype(BF16)

    q = jnp.dot(h2, wq_ref[...], preferred_element_type=F32)
    q_gate = jnp.dot(packed, pq_ref[...], preferred_element_type=F32)
    q_ones = _rows(cq_ref[...])
    for r in range(0, tm, VEC_ROWS):
        qa_ref[0, r:r + VEC_ROWS, :] = (q[r:r + VEC_ROWS] + q_gate[r:r + VEC_ROWS] + q_ones).astype(BF16)
    mvt_ref[0] = mv.T.astype(BF16)

    k = jnp.dot(h2, wk_ref[...], preferred_element_type=F32)
    k_gate = jnp.dot(packed, pk_ref[...], preferred_element_type=F32)
    k_ones = _rows(ck_ref[...])
    for r in range(0, tm, VEC_ROWS):
        ka_ref[0, r:r + VEC_ROWS, :] = (k[r:r + VEC_ROWS] + k_gate[r:r + VEC_ROWS] + k_ones).astype(BF16)

    gtok_ref[0] = u_gate
    gt_ref[0, 0:8, :] = a_loc.T[MLSTM_GATE_LANE:MLSTM_GATE_LANE + 8, :]
    gt_ref[0, 8:16, :] = u_gate.T[MLSTM_GATE_LANE:MLSTM_GATE_LANE + 8, :]


def _pre(x, mod, w1i, w1o, wq, wk, wv, wmqk, wmv, wmo, wg, gbias, pq, pk, cq, ck, convw, convb, kscale):
    bsz, seq, _ = x.shape
    tm = ROW_TILE
    grid = (bsz, seq // tm)
    tok = lambda width: pl.BlockSpec((1, tm, width), lambda b, j: (b, j, 0))
    feat = lambda rows: pl.BlockSpec((1, rows, tm), lambda b, j: (b, 0, j))
    consts = (w1i, w1o, wq, wk, wv, wmqk, wmv, wmo, wg, gbias, pq, pk, cq, ck, convw, convb, kscale)
    out_shape = (
        jax.ShapeDtypeStruct((bsz, seq, D_MODEL), F32),
        jax.ShapeDtypeStruct((bsz, seq, FOX_HEADS * FOX_GROUP), BF16),
        jax.ShapeDtypeStruct((bsz, seq, FOX_HEADS * FOX_GROUP), BF16),
        jax.ShapeDtypeStruct((bsz, FOX_WIDTH, seq), BF16),
        jax.ShapeDtypeStruct((bsz, seq, 2 * MLSTM_WIDTH), BF16),
        jax.ShapeDtypeStruct((bsz, MLSTM_WIDTH, seq), BF16),
        jax.ShapeDtypeStruct((bsz, seq, MLSTM_WIDTH), F32),
        jax.ShapeDtypeStruct((bsz, seq, LANES), F32),
        jax.ShapeDtypeStruct((bsz, 16, seq), F32),
    )
    out_specs = (tok(D_MODEL), tok(FOX_HEADS * FOX_GROUP), tok(FOX_HEADS * FOX_GROUP), feat(FOX_WIDTH),
                 tok(2 * MLSTM_WIDTH), feat(MLSTM_WIDTH), tok(MLSTM_WIDTH), tok(LANES), feat(16))
    return pl.pallas_call(
        functools.partial(_pre_kernel, tm=tm, lm=MLSTM_CHUNK),
        grid=grid,
        in_specs=[tok(D_MODEL), pl.BlockSpec((1, N_MOD, D_MODEL), lambda b, j: (b, 0, 0))]
        + [_resident(a.shape) for a in consts],
        out_specs=out_specs,
        out_shape=out_shape,
        scratch_shapes=[pltpu.VMEM((8, LANES), F32), pltpu.VMEM((tm + 8, 2 * MLSTM_WIDTH), F32)],
        compiler_params=pltpu.CompilerParams(dimension_semantics=("arbitrary", "arbitrary"),
                                             vmem_limit_bytes=VMEM_LIMIT),
        name="pre",
    )(x, mod, *consts)


def _fox_kernel(q_ref, k_ref, vt_ref, o_ref, s_scr, p_scr, m_scr, alpha_scr, acc_scr, *, tq):
    qi = pl.program_id(2)
    heads = range(FOX_STEP_HEADS)
    lanes = [slice(FOX_GROUP * h, FOX_GROUP * (h + 1)) for h in heads]
    rows = [slice(FOX_HEAD_DIM * h, FOX_HEAD_DIM * (h + 1)) for h in heads]

    def put_scores(h, k0):
        s_scr[h] = lax.dot_general(k_ref[0, pl.ds(k0, tq), lanes[h]], q_ref[0, :, lanes[h]],
                                   (((1,), (1,)), ((), ())), preferred_element_type=F32)

    def softmax(h, masked):
        def block(r):
            s = s_scr[h, r:r + FOX_ROW_BLOCK, :]
            if masked:
                kpos = r + lax.broadcasted_iota(jnp.int32, (FOX_ROW_BLOCK, tq), 0)
                qpos = lax.broadcasted_iota(jnp.int32, (FOX_ROW_BLOCK, tq), 1)
                s = jnp.where(kpos <= qpos, s, -jnp.inf)
            return s

        top = block(0)
        for r in range(FOX_ROW_BLOCK, tq, FOX_ROW_BLOCK):
            top = jnp.maximum(top, block(r))
        m = m_scr[h, 0:1, :]
        m_new = jnp.maximum(m, jnp.max(top, axis=0, keepdims=True))
        alpha_scr[h, 0:1, :] = jnp.exp2(m - m_new)
        m_scr[h, 0:1, :] = m_new
        for r in range(0, tq, FOX_ROW_BLOCK):
            p_scr[h, r:r + FOX_ROW_BLOCK, :] = jnp.exp2(block(r) - m_new).astype(BF16)

    def weighted_values(h, k0):
        v_aug = jnp.concatenate([vt_ref[0, rows[h], pl.ds(k0, tq)], jnp.ones((16, tq), BF16)], axis=0)
        acc_scr[h] = alpha_scr[h, 0:1, :] * acc_scr[h] + jnp.dot(v_aug, p_scr[h], preferred_element_type=F32)

    def step(t, masked):
        k0 = pl.multiple_of(t * tq, tq)
        k_prev = pl.multiple_of(jnp.maximum(k0 - tq, 0), tq)
        for g in range(0, FOX_STEP_HEADS, FOX_ISSUE_HEADS):
            group = range(g, g + FOX_ISSUE_HEADS)
            for h in group:
                put_scores(h, k0)
            for h in group:
                weighted_values(h, k_prev)
            for h in group:
                softmax(h, masked)

    m_scr[...] = jnp.full(m_scr.shape, -jnp.inf, F32)
    alpha_scr[...] = jnp.ones(alpha_scr.shape, F32)
    acc_scr[...] = jnp.zeros(acc_scr.shape, F32)
    p_scr[...] = jnp.zeros(p_scr.shape, BF16)

    @pl.loop(0, qi)
    def _(t):
        step(t, False)

    step(qi, True)
    outs = []
    for h in heads:
        weighted_values(h, pl.multiple_of(qi * tq, tq))
        a = acc_scr[h]
        outs.append(a[0:FOX_HEAD_DIM] / a[FOX_HEAD_DIM:FOX_HEAD_DIM + 1])
    o_ref[0] = jnp.concatenate(outs, axis=0).T.astype(BF16)


def _fox(qa, ka, fvt):
    bsz, seq, _ = qa.shape
    tq, nh = FOX_TILE, FOX_STEP_HEADS
    return pl.pallas_call(
        functools.partial(_fox_kernel, tq=tq),
        grid=(bsz, FOX_HEADS // nh, seq // tq),
        in_specs=[pl.BlockSpec((1, tq, nh * FOX_GROUP), lambda b, g, i: (b, i, g)),
                  pl.BlockSpec((1, seq, nh * FOX_GROUP), lambda b, g, i: (b, 0, g)),
                  pl.BlockSpec((1, nh * FOX_HEAD_DIM, seq), lambda b, g, i: (b, g, 0))],
        out_specs=pl.BlockSpec((1, tq, nh * FOX_HEAD_DIM), lambda b, g, i: (b, i, g)),
        out_shape=jax.ShapeDtypeStruct((bsz, seq, FOX_WIDTH), BF16),
        scratch_shapes=[pltpu.VMEM((nh, tq, tq), F32), pltpu.VMEM((nh, tq, tq), BF16),
                        pltpu.VMEM((nh, 8, tq), F32), pltpu.VMEM((nh, 8, tq), F32),
                        pltpu.VMEM((nh, FOX_HEAD_DIM + 16, tq), F32)],
        compiler_params=pltpu.CompilerParams(dimension_semantics=("arbitrary",) * 3,
                                             vmem_limit_bytes=VMEM_LIMIT),
        name="fox",
    )(qa, ka, fvt)


def _mlstm_kernel(q_ref, k_ref, vt_ref, mo_ref, gtok_ref, gt_ref, gain_ref, o_ref, c_scr, m_scr, *, ts, lm):
    @pl.when(pl.program_id(1) == 0)
    def _():
        c_scr[...] = jnp.zeros_like(c_scr)
        m_scr[...] = jnp.zeros_like(m_scr)

    s_idx = lax.broadcasted_iota(jnp.int32, (lm, lm), 0)
    t_idx = lax.broadcasted_iota(jnp.int32, (lm, lm), 1)
    causal = s_idx <= t_idx
    dh = MLSTM_HEAD_DIM
    for c in range(ts // lm):
        tok = slice(c * lm, (c + 1) * lm)
        for h in range(MLSTM_HEADS):
            cols = slice(dh * h, dh * (h + 1))
            q = q_ref[0, tok, cols]
            k = k_ref[0, tok, cols]
            vt = vt_ref[0, cols, tok]
            a_row = gt_ref[0, h:h + 1, tok]
            u_row = gt_ref[0, 8 + h:9 + h, tok]
            u_col = gtok_ref[0, tok, MLSTM_GATE_LANE + h:MLSTM_GATE_LANE + h + 1]
            m_prev = m_scr[h:h + 1, 0:1]
            c_prev = c_scr[h]

            g = a_row[:, lm - 1:lm]
            w_end = g + u_row
            m_new = jnp.maximum(g + m_prev, jnp.max(w_end, axis=1, keepdims=True))
            decay = jnp.exp(g + m_prev - m_new)
            wt = jnp.exp(w_end - m_new)

            d_log = jnp.where(causal, a_row + u_col, -jnp.inf)
            inter = a_row + m_prev
            m_t = jnp.maximum(inter, jnp.max(d_log, axis=0, keepdims=True))
            p = jnp.exp(d_log - m_t)
            inter_w = jnp.exp(inter - m_t)

            scores = lax.dot_general(k, q, (((1,), (1,)), ((), ())), preferred_element_type=F32) * p
            cq = lax.dot_general(c_prev.astype(BF16), q, (((1,), (1,)), ((), ())),
                                 preferred_element_type=F32)
            num = jnp.dot(vt, scores.astype(BF16), preferred_element_type=F32) + inter_w * cq[0:dh]
            den = jnp.sum(scores, axis=0, keepdims=True) + inter_w * cq[dh:dh + 1]
            out_t = num / jnp.maximum(jnp.abs(den), jnp.exp(-m_t))
            hn = out_t * lax.rsqrt(jnp.mean(out_t * out_t, axis=0, keepdims=True) + EPS)
            hb = hn.T * gain_ref[:, cols]
            o_ref[0, tok, cols] = (jax.nn.sigmoid(mo_ref[0, tok, cols]) * hb).astype(BF16)

            v_w = jnp.concatenate([vt.astype(F32) * wt, jnp.broadcast_to(wt, (16, lm))], axis=0).astype(BF16)
            c_scr[h] = decay * c_prev + jnp.dot(v_w, k, preferred_element_type=F32)
            m_scr[h:h + 1, :] = jnp.broadcast_to(m_new, (1, LANES))


def _mlstm(mqk, mvt, mo, gtok, gt, gain):
    bsz, seq, _ = mo.shape
    ts, lm = MLSTM_SEQ_TILE, MLSTM_CHUNK
    return pl.pallas_call(
        functools.partial(_mlstm_kernel, ts=ts, lm=lm),
        grid=(bsz, seq // ts),
        in_specs=[pl.BlockSpec((1, ts, MLSTM_WIDTH), lambda b, i: (b, i, 0)),
                  pl.BlockSpec((1, ts, MLSTM_WIDTH), lambda b, i: (b, i, 1)),
                  pl.BlockSpec((1, MLSTM_WIDTH, ts), lambda b, i: (b, 0, i)),
                  pl.BlockSpec((1, ts, MLSTM_WIDTH), lambda b, i: (b, i, 0)),
                  pl.BlockSpec((1, ts, LANES), lambda b, i: (b, i, 0)),
                  pl.BlockSpec((1, 16, ts), lambda b, i: (b, 0, i)),
                  pl.BlockSpec((1, MLSTM_WIDTH), lambda b, i: (0, 0))],
        out_specs=pl.BlockSpec((1, ts, MLSTM_WIDTH), lambda b, i: (b, i, 0)),
        out_shape=jax.ShapeDtypeStruct((bsz, seq, MLSTM_WIDTH), BF16),
        scratch_shapes=[pltpu.VMEM((MLSTM_HEADS, MLSTM_STATE_ROWS, MLSTM_HEAD_DIM), F32),
                        pltpu.VMEM((8, LANES), F32)],
        compiler_params=pltpu.CompilerParams(dimension_semantics=("arbitrary", "arbitrary"),
                                             vmem_limit_bytes=VMEM_LIMIT),
        name="mlstm",
    )(mqk, mqk, mvt, mo, gtok, gt, gain)


def _post_kernel(x1_ref, ya_ref, yb_ref, mod_ref, woa_ref, wob_ref, w2i_ref, w2o_ref, gf_ref, o_ref, x2_scr):
    gt2, sh3 = _rows(mod_ref[0, 5:6, :]), _rows(mod_ref[0, 6:7, :])
    sc3p, gt3h, gain = _rows(1 + mod_ref[0, 7:8, :]), _rows(0.5 * mod_ref[0, 8:9, :]), _rows(gf_ref[...])
    tm = x1_ref.shape[1]
    ya = jnp.dot(ya_ref[0], woa_ref[...], preferred_element_type=F32)
    yb = jnp.dot(yb_ref[0], wob_ref[...], preferred_element_type=F32)

    def residual_and_norm(r):
        x2 = x1_ref[0, r, :] + gt2 * (ya[r] + yb[r])
        x2_scr[r, :] = x2
        return (_rms(x2) * sc3p + sh3).astype(BF16)

    h = _by_rows(residual_and_norm, tm, VEC_ROWS)
    ffn2 = _ffn(h, w2i_ref, w2o_ref)
    for r in range(0, tm, VEC_ROWS):
        x3 = x2_scr[r:r + VEC_ROWS, :] + gt3h * ffn2[r:r + VEC_ROWS]
        o_ref[0, r:r + VEC_ROWS, :] = _rms(x3) * gain


def _post(x1, ya, yb, mod, woa, wob, w2i, w2o, gf):
    bsz, seq, _ = x1.shape
    tm = ROW_TILE
    tok = lambda width: pl.BlockSpec((1, tm, width), lambda b, j: (b, j, 0))
    consts = (woa, wob, w2i, w2o, gf)
    return pl.pallas_call(
        _post_kernel,
        grid=(bsz, seq // tm),
        in_specs=[tok(D_MODEL), tok(FOX_WIDTH), tok(MLSTM_WIDTH),
                  pl.BlockSpec((1, N_MOD, D_MODEL), lambda b, j: (b, 0, 0))]
        + [_resident(a.shape) for a in consts],
        out_specs=tok(D_MODEL),
        out_shape=jax.ShapeDtypeStruct((bsz, seq, D_MODEL), F32),
        scratch_shapes=[pltpu.VMEM((tm, D_MODEL), F32)],
        compiler_params=pltpu.CompilerParams(dimension_semantics=("arbitrary", "arbitrary"),
                                             vmem_limit_bytes=VMEM_LIMIT),
        name="post",
    )(x1, ya, yb, mod, *consts)


def _fox_head_groups(w):
    w = w.reshape(D_MODEL, FOX_HEADS, FOX_HEAD_DIM)
    w = jnp.pad(w, ((0, 0), (0, 0), (0, FOX_GROUP - FOX_HEAD_DIM)))
    return w.reshape(D_MODEL, FOX_HEADS * FOX_GROUP)


def _gate_placement():
    pq = np.zeros((LANES, FOX_HEADS * FOX_GROUP), np.float32)
    pk = np.zeros((LANES, FOX_HEADS * FOX_GROUP), np.float32)
    cq = np.zeros((1, FOX_HEADS * FOX_GROUP), np.float32)
    ck = np.zeros((1, FOX_HEADS * FOX_GROUP), np.float32)
    for h in range(FOX_HEADS):
        base = FOX_GROUP * h + GATE_LANE
        for part in range(3):
            pq[8 * part + h, base + part] = 1.0
            pk[8 * part + h, base + 3 + part] = -1.0
            cq[0, base + 3 + part] = 1.0
            ck[0, base + part] = 1.0
    return jnp.asarray(pq, BF16), jnp.asarray(pk, BF16), jnp.asarray(cq), jnp.asarray(ck)


def kernel(x, c, w_ada, b_ada, w_ffn1_in, w_ffn1_out, w_mix_in, b_fox_f, b_m_i, b_m_f, w_conv, b_conv, g_mhn,
           w_mix_out, w_ffn2_in, w_ffn2_out, g_final):
    assert w_ada.shape[0] == 1, "one layer"
    bsz, seq, _ = x.shape
    assert seq % max(ROW_TILE, MLSTM_SEQ_TILE, FOX_TILE) == 0

    mod = _ada(c, w_ada[0], b_ada[0]).reshape(bsz, N_MOD, D_MODEL)

    w_mix = w_mix_in[0]
    o = 0
    cols = {}
    for name, width in (("fq", FOX_WIDTH), ("fk", FOX_WIDTH), ("fv", FOX_WIDTH), ("ff", FOX_HEADS),
                        ("mq", MLSTM_WIDTH), ("mk", MLSTM_WIDTH), ("mv", MLSTM_WIDTH), ("mo", MLSTM_WIDTH),
                        ("mi", MLSTM_HEADS), ("mf", MLSTM_HEADS)):
        cols[name] = w_mix[:, o:o + width]
        o += width
    wq = _fox_head_groups(cols["fq"] * (FOX_HEAD_DIM ** -0.5 * LOG2_E)).astype(BF16)
    wk = _fox_head_groups(cols["fk"]).astype(BF16)
    wv = cols["fv"].astype(BF16)
    wmqk = jnp.concatenate([cols["mq"], cols["mk"]], axis=1).astype(BF16)
    wmv = cols["mv"].astype(BF16)
    wmo = cols["mo"].astype(BF16)
    zeros = lambda n: jnp.zeros((D_MODEL, n), F32)
    wg = jnp.concatenate([cols["ff"], cols["ff"], cols["ff"], cols["mi"], zeros(LANES - 28),
                          zeros(MLSTM_GATE_LANE), cols["mf"], zeros(LANES - 28)], axis=1).astype(BF16)
    zrow = lambda n: jnp.zeros((n,), F32)
    gbias = jnp.concatenate([b_fox_f[0], b_fox_f[0], b_fox_f[0], b_m_i[0], zrow(LANES - 28),
                             zrow(MLSTM_GATE_LANE), b_m_f[0], zrow(LANES - 28)]).reshape(1, 2 * LANES)
    pq, pk, cq, ck = _gate_placement()
    kscale = jnp.concatenate([jnp.ones((MLSTM_WIDTH,), F32),
                              jnp.full((MLSTM_WIDTH,), MLSTM_HEAD_DIM ** -0.5, F32)]).reshape(1, 2 * MLSTM_WIDTH)

    x1, qa, ka, fvt, mqk, mvt, mo, gtok, gt = _pre(
        x, mod, w_ffn1_in[0].astype(BF16), w_ffn1_out[0].astype(BF16), wq, wk, wv, wmqk, wmv, wmo, wg, gbias,
        pq, pk, cq, ck, w_conv[0], b_conv[0].reshape(1, -1), kscale)

    ya = _fox(qa, ka, fvt)
    yb = _mlstm(mqk, mvt, mo, gtok, gt, g_mhn[0].reshape(1, MLSTM_WIDTH))

    w_out = w_mix_out[0].astype(BF16)
    return _post(x1, ya, yb, mod, w_out[:FOX_WIDTH], w_out[FOX_WIDTH:], w_ffn2_in[0].astype(BF16),
                 w_ffn2_out[0].astype(BF16), g_final.reshape(1, D_MODEL))
```

```python
import functools

import jax
import jax.numpy as jnp
import numpy as np
from jax import lax
from jax.experimental import pallas as pl
from jax.experimental.pallas import tpu as pltpu

F32 = jnp.float32
BF16 = jnp.bfloat16

D_MODEL = 1024
D_FF = 2816
N_MOD = 9
EPS = 1e-6
LOG2_E = 1.4426950408889634
FOX_HEADS = 8
FOX_HEAD_DIM = 64
FOX_WIDTH = FOX_HEADS * FOX_HEAD_DIM
MLSTM_HEADS = 4
MLSTM_HEAD_DIM = 128
MLSTM_WIDTH = MLSTM_HEADS * MLSTM_HEAD_DIM
CONV_WIDTH = 4

LANES = 128
VEC_ROWS = 16
FOX_GROUP = 128
GATE_LANE = FOX_HEAD_DIM
MLSTM_GATE_LANE = 24
MLSTM_STATE_ROWS = MLSTM_HEAD_DIM + 16

ROW_TILE = 512
FFN_CHUNKS = (512, 512, 512, 512, 512, 256)
MLSTM_CHUNK = 256
MLSTM_SEQ_TILE = 512
FOX_TILE = 512
FOX_STEP_HEADS = 8
FOX_ISSUE_HEADS = 2
FOX_ROW_BLOCK = 32
VMEM_LIMIT = 56 * 1024 * 1024


def _rms(x):
    return x * lax.rsqrt(jnp.mean(x * x, axis=-1, keepdims=True) + EPS)


def _by_rows(fn, rows, step):
    return jnp.concatenate([fn(slice(r, r + step)) for r in range(0, rows, step)], axis=0)


def _rows(row):
    return jnp.broadcast_to(row, (VEC_ROWS, row.shape[-1]))


def _split3(x):
    hi = x.astype(BF16)
    r1 = x - hi.astype(F32)
    mid = r1.astype(BF16)
    lo = (r1 - mid.astype(F32)).astype(BF16)
    return hi, mid, lo


def _ffn(h, w_in_ref, w_out_ref):
    acc = None
    j0 = 0
    for width in FFN_CHUNKS:
        g = jnp.dot(h, w_in_ref[:, j0:j0 + width], preferred_element_type=F32)
        u = jnp.dot(h, w_in_ref[:, D_FF + j0:D_FF + j0 + width], preferred_element_type=F32)
        a = _by_rows(lambda r: (jax.nn.silu(g[r]) * u[r]).astype(BF16), g.shape[0], 2 * VEC_ROWS)
        part = jnp.dot(a, w_out_ref[j0:j0 + width, :], preferred_element_type=F32)
        acc = part if acc is None else acc + part
        j0 += width
    return acc


def _resident(shape):
    zeros = (0,) * len(shape)
    return pl.BlockSpec(shape, lambda *_: zeros, pipeline_mode=pl.Buffered(1))


def _ada_kernel(c_ref, w_ref, b_ref, o_ref):
    c_act = jax.nn.silu(c_ref[...])
    o_ref[...] = jnp.dot(c_act, w_ref[...], preferred_element_type=F32,
                         precision=lax.Precision.HIGHEST) + b_ref[...]


def _ada(c, w, b):
    bsz = c.shape[0]
    rows = 8
    c_pad = jnp.zeros((rows, D_MODEL), F32).at[:bsz].set(c)
    n = w.shape[1]
    tn = 1536
    out = pl.pallas_call(
        _ada_kernel,
        grid=(n // tn,),
        in_specs=[pl.BlockSpec((rows, D_MODEL), lambda j: (0, 0)),
                  pl.BlockSpec((D_MODEL, tn), lambda j: (0, j)),
                  pl.BlockSpec((1, tn), lambda j: (0, j))],
        out_specs=pl.BlockSpec((rows, tn), lambda j: (0, j)),
        out_shape=jax.ShapeDtypeStruct((rows, n), F32),
        compiler_params=pltpu.CompilerParams(dimension_semantics=("arbitrary",),
                                             vmem_limit_bytes=VMEM_LIMIT),
        name="ada",
    )(c_pad, w, b.reshape(1, n))
    return out[:bsz]


def _pre_kernel(x_ref, mod_ref, w1i_ref, w1o_ref, wq_ref, wk_ref, wv_ref, wmqk_ref, wmv_ref, wmo_ref,
                wg_ref, gbias_ref, pq_ref, pk_ref, cq_ref, ck_ref, convw_ref, convb_ref, kscale_ref,
                x1_ref, qa_ref, ka_ref, fvt_ref, mqk_ref, mvt_ref, mo_ref, gtok_ref, gt_ref,
                cum_carry, convbuf, *, tm, lm):
    j = pl.program_id(1)
    sh1, sc1p, gt1h = _rows(mod_ref[0, 0:1, :]), _rows(1 + mod_ref[0, 1:2, :]), _rows(0.5 * mod_ref[0, 2:3, :])
    sh2, sc2p = _rows(mod_ref[0, 3:4, :]), _rows(1 + mod_ref[0, 4:5, :])

    row = lax.broadcasted_iota(jnp.int32, (tm, tm), 0)
    col = lax.broadcasted_iota(jnp.int32, (tm, tm), 1)
    dist = row - col
    tri_seq = jnp.where(dist >= 0, 1.0, 0.0).astype(BF16)
    tri_chunk = jnp.where(dist >= 0, jnp.where(dist <= (row & (lm - 1)), 1.0, 0.0), 0.0).astype(BF16)

    def cumsum_rows(tri, v):
        hi, mid, lo = _split3(v)
        r = jnp.dot(tri, jnp.concatenate([hi, mid, lo], axis=1), preferred_element_type=F32)
        return (r[:, :LANES] + r[:, LANES:2 * LANES]) + r[:, 2 * LANES:]

    h = _by_rows(lambda r: (_rms(x_ref[0, r, :]) * sc1p + sh1).astype(BF16), tm, VEC_ROWS)
    ffn1 = _ffn(h, w1i_ref, w1o_ref)

    def residual_and_norm(r):
        x1 = x_ref[0, r, :] + gt1h * ffn1[r]
        x1_ref[0, r, :] = x1
        return (_rms(x1) * sc2p + sh2).astype(BF16)

    h2 = _by_rows(residual_and_norm, tm, VEC_ROWS)

    z = jnp.dot(h2, wg_ref[...], preferred_element_type=F32) + gbias_ref[...]
    mqk = jnp.dot(h2, wmqk_ref[...], preferred_element_type=F32)

    za, zb = z[:, :LANES], z[:, LANES:]
    ls_a = jax.nn.log_sigmoid(za)
    ls_b = jax.nn.log_sigmoid(zb)

    @pl.when(j == 0)
    def _():
        cum_carry[...] = jnp.zeros_like(cum_carry)
        convbuf[0:8, :] = jnp.zeros((8, 2 * MLSTM_WIDTH), F32)

    @pl.when(j > 0)
    def _():
        convbuf[0:8, :] = convbuf[tm:tm + 8, :]

    v = jnp.dot(h2, wv_ref[...], preferred_element_type=F32)

    cum_f = cumsum_rows(tri_seq, ls_a) + cum_carry[0:1, :]
    cum_carry[0:1, :] = cum_f[tm - 1:tm, :]
    a_loc = cumsum_rows(tri_chunk, ls_b)
    u_gate = za - a_loc

    lane = lax.broadcasted_iota(jnp.int32, (tm, LANES), 1)
    cum_2 = cum_f * LOG2_E
    c_hi = cum_2.astype(BF16).astype(F32)
    c_mid = (cum_2 - c_hi).astype(BF16).astype(F32)
    c_lo = (cum_2 - c_hi) - c_mid
    packed = jnp.where(lane < 8, c_hi, jnp.where(lane < 16, c_mid, jnp.where(lane < 24, c_lo, 0.0))).astype(BF16)

    mv = jnp.dot(h2, wmv_ref[...], preferred_element_type=F32)

    convbuf[8:tm + 8, :] = mqk
    for c in range(0, 2 * MLSTM_WIDTH, MLSTM_WIDTH):
        cols = slice(c, c + MLSTM_WIDTH)
        taps = [_rows(convw_ref[t:t + 1, cols]) for t in range(CONV_WIDTH)]
        bias, out_scale = _rows(convb_ref[:, cols]), _rows(kscale_ref[:, cols])
        for r in range(0, tm, VEC_ROWS):
            y = bias + taps[3] * convbuf[8 + r:8 + r + VEC_ROWS, cols]
            y = y + taps[2] * convbuf[7 + r:7 + r + VEC_ROWS, cols]
            y = y + taps[1] * convbuf[6 + r:6 + r + VEC_ROWS, cols]
            y = y + taps[0] * convbuf[5 + r:5 + r + VEC_ROWS, cols]
            mqk_ref[0, r:r + VEC_ROWS, cols] = (jax.nn.silu(y) * out_scale).astype(BF16)

    mo_ref[0] = jnp.dot(h2, wmo_ref[...], preferred_element_type=F32)
    fvt_ref[0] = v.T.astype(BF16)

    head_lane = lax.broadcasted_iota(jnp.int32, (VEC_ROWS, FOX_GROUP), 1) < FOX_HEAD_DIM

    def fox_groups(proj, gate, ones, r):
        groups = []
        for pair in range(FOX_HEADS // 2):
            both = proj[r, FOX_GROUP * pair:FOX_GROUP * (pair + 1)]
            for head, vals in ((2 * pair, both), (2 * pair + 1, pltpu.roll(both, FOX_HEAD_DIM, axis=1))):
                lanes = slice(FOX_GROUP * head, FOX_GROUP * (head + 1))
                groups.append(jnp.where(head_lane, vals, gate[r, lanes] + ones[:, lanes]))
        return jnp.concatenate(groups, axis=1).astype(BF16)

    q = jnp.dot(h2, wq_ref[...], preferred_element_type=F32)
    q_gate = jnp.dot(packed, pq_ref[...], preferred_element_type=F32)
    q_ones = _rows(cq_ref[...])
    for r in range(0, tm, VEC_ROWS):
        qa_ref[0, r:r + VEC_ROWS, :] = fox_groups(q, q_gate, q_ones, slice(r, r + VEC_ROWS))
    mvt_ref[0] = mv.T.astype(BF16)

    k = jnp.dot(h2, wk_ref[...], preferred_element_type=F32)
    k_gate = jnp.dot(packed, pk_ref[...], preferred_element_type=F32)
    k_ones = _rows(ck_ref[...])
    for r in range(0, tm, VEC_ROWS):
        ka_ref[0, r:r + VEC_ROWS, :] = fox_groups(k, k_gate, k_ones, slice(r, r + VEC_ROWS))

    gtok_ref[0] = u_gate
    gt_ref[0, 0:8, :] = a_loc.T[MLSTM_GATE_LANE:MLSTM_GATE_LANE + 8, :]
    gt_ref[0, 8:16, :] = u_gate.T[MLSTM_GATE_LANE:MLSTM_GATE_LANE + 8, :]


def _pre(x, mod, w1i, w1o, wq, wk, wv, wmqk, wmv, wmo, wg, gbias, pq, pk, cq, ck, convw, convb, kscale):
    bsz, seq, _ = x.shape
    tm = ROW_TILE
    grid = (bsz, seq // tm)
    tok = lambda width: pl.BlockSpec((1, tm, width), lambda b, j: (b, j, 0))
    feat = lambda rows: pl.BlockSpec((1, rows, tm), lambda b, j: (b, 0, j))
    consts = (w1i, w1o, wq, wk, wv, wmqk, wmv, wmo, wg, gbias, pq, pk, cq, ck, convw, convb, kscale)
    out_shape = (
        jax.ShapeDtypeStruct((bsz, seq, D_MODEL), F32),
        jax.ShapeDtypeStruct((bsz, seq, FOX_HEADS * FOX_GROUP), BF16),
        jax.ShapeDtypeStruct((bsz, seq, FOX_HEADS * FOX_GROUP), BF16),
        jax.ShapeDtypeStruct((bsz, FOX_WIDTH, seq), BF16),
        jax.ShapeDtypeStruct((bsz, seq, 2 * MLSTM_WIDTH), BF16),
        jax.ShapeDtypeStruct((bsz, MLSTM_WIDTH, seq), BF16),
        jax.ShapeDtypeStruct((bsz, seq, MLSTM_WIDTH), F32),
        jax.ShapeDtypeStruct((bsz, seq, LANES), F32),
        jax.ShapeDtypeStruct((bsz, 16, seq), F32),
    )
    out_specs = (tok(D_MODEL), tok(FOX_HEADS * FOX_GROUP), tok(FOX_HEADS * FOX_GROUP), feat(FOX_WIDTH),
                 tok(2 * MLSTM_WIDTH), feat(MLSTM_WIDTH), tok(MLSTM_WIDTH), tok(LANES), feat(16))
    return pl.pallas_call(
        functools.partial(_pre_kernel, tm=tm, lm=MLSTM_CHUNK),
        grid=grid,
        in_specs=[tok(D_MODEL), pl.BlockSpec((1, N_MOD, D_MODEL), lambda b, j: (b, 0, 0))]
        + [_resident(a.shape) for a in consts],
        out_specs=out_specs,
        out_shape=out_shape,
        scratch_shapes=[pltpu.VMEM((8, LANES), F32), pltpu.VMEM((tm + 8, 2 * MLSTM_WIDTH), F32)],
        compiler_params=pltpu.CompilerParams(dimension_semantics=("arbitrary", "arbitrary"),
                                             vmem_limit_bytes=VMEM_LIMIT),
        name="pre",
    )(x, mod, *consts)


def _fox_kernel(q_ref, k_ref, vt_ref, o_ref, s_scr, p_scr, m_scr, alpha_scr, acc_scr, *, tq):
    qi = pl.program_id(2)
    heads = range(FOX_STEP_HEADS)
    lanes = [slice(FOX_GROUP * h, FOX_GROUP * (h + 1)) for h in heads]
    rows = [slice(FOX_HEAD_DIM * h, FOX_HEAD_DIM * (h + 1)) for h in heads]

    def put_scores(h, k0):
        s_scr[h] = lax.dot_general(k_ref[0, pl.ds(k0, tq), lanes[h]], q_ref[0, :, lanes[h]],
                                   (((1,), (1,)), ((), ())), preferred_element_type=F32)

    def softmax(h, masked):
        def block(r):
            s = s_scr[h, r:r + FOX_ROW_BLOCK, :]
            if masked:
                kpos = r + lax.broadcasted_iota(jnp.int32, (FOX_ROW_BLOCK, tq), 0)
                qpos = lax.broadcasted_iota(jnp.int32, (FOX_ROW_BLOCK, tq), 1)
                s = jnp.where(kpos <= qpos, s, -jnp.inf)
            return s

        top = block(0)
        for r in range(FOX_ROW_BLOCK, tq, FOX_ROW_BLOCK):
            top = jnp.maximum(top, block(r))
        m = m_scr[h, 0:1, :]
        m_new = jnp.maximum(m, jnp.max(top, axis=0, keepdims=True))
        alpha_scr[h, 0:1, :] = jnp.exp2(m - m_new)
        m_scr[h, 0:1, :] = m_new
        for r in range(0, tq, FOX_ROW_BLOCK):
            p_scr[h, r:r + FOX_ROW_BLOCK, :] = jnp.exp2(block(r) - m_new).astype(BF16)

    def weighted_values(h, k0):
        v_aug = jnp.concatenate([vt_ref[0, rows[h], pl.ds(k0, tq)], jnp.ones((16, tq), BF16)], axis=0)
        acc_scr[h] = alpha_scr[h, 0:1, :] * acc_scr[h] + jnp.dot(v_aug, p_scr[h], preferred_element_type=F32)

    def step(t, masked):
        k0 = pl.multiple_of(t * tq, tq)
        k_prev = pl.multiple_of(jnp.maximum(k0 - tq, 0), tq)
        for g in range(0, FOX_STEP_HEADS, FOX_ISSUE_HEADS):
            group = range(g, g + FOX_ISSUE_HEADS)
            for h in group:
                put_scores(h, k0)
            for h in group:
                weighted_values(h, k_prev)
            for h in group:
                softmax(h, masked)

    m_scr[...] = jnp.full(m_scr.shape, -jnp.inf, F32)
    alpha_scr[...] = jnp.ones(alpha_scr.shape, F32)
    acc_scr[...] = jnp.zeros(acc_scr.shape, F32)
    p_scr[...] = jnp.zeros(p_scr.shape, BF16)

    @pl.loop(0, qi)
    def _(t):
        step(t, False)

    step(qi, True)
    outs = []
    for h in heads:
        weighted_values(h, pl.multiple_of(qi * tq, tq))
        a = acc_scr[h]
        outs.append(a[0:FOX_HEAD_DIM] / a[FOX_HEAD_DIM:FOX_HEAD_DIM + 1])
    o_ref[0] = jnp.concatenate(outs, axis=0).T.astype(BF16)


def _fox(qa, ka, fvt):
    bsz, seq, _ = qa.shape
    tq, nh = FOX_TILE, FOX_STEP_HEADS
    return pl.pallas_call(
        functools.partial(_fox_kernel, tq=tq),
        grid=(bsz, FOX_HEADS // nh, seq // tq),
        in_specs=[pl.BlockSpec((1, tq, nh * FOX_GROUP), lambda b, g, i: (b, i, g)),
                  pl.BlockSpec((1, seq, nh * FOX_GROUP), lambda b, g, i: (b, 0, g)),
                  pl.BlockSpec((1, nh * FOX_HEAD_DIM, seq), lambda b, g, i: (b, g, 0))],
        out_specs=pl.BlockSpec((1, tq, nh * FOX_HEAD_DIM), lambda b, g, i: (b, i, g)),
        out_shape=jax.ShapeDtypeStruct((bsz, seq, FOX_WIDTH), BF16),
        scratch_shapes=[pltpu.VMEM((nh, tq, tq), F32), pltpu.VMEM((nh, tq, tq), BF16),
                        pltpu.VMEM((nh, 8, tq), F32), pltpu.VMEM((nh, 8, tq), F32),
                        pltpu.VMEM((nh, FOX_HEAD_DIM + 16, tq), F32)],
        compiler_params=pltpu.CompilerParams(dimension_semantics=("arbitrary",) * 3,
                                             vmem_limit_bytes=VMEM_LIMIT),
        name="fox",
    )(qa, ka, fvt)


def _mlstm_kernel(q_ref, k_ref, vt_ref, mo_ref, gtok_ref, gt_ref, gain_ref, o_ref, c_scr, m_scr, *, ts, lm):
    @pl.when(pl.program_id(1) == 0)
    def _():
        c_scr[...] = jnp.zeros_like(c_scr)
        m_scr[...] = jnp.zeros_like(m_scr)

    s_idx = lax.broadcasted_iota(jnp.int32, (lm, lm), 0)
    t_idx = lax.broadcasted_iota(jnp.int32, (lm, lm), 1)
    causal = s_idx <= t_idx
    dh = MLSTM_HEAD_DIM
    for c in range(ts // lm):
        tok = slice(c * lm, (c + 1) * lm)
        for h in range(MLSTM_HEADS):
            cols = slice(dh * h, dh * (h + 1))
            q = q_ref[0, tok, cols]
            k = k_ref[0, tok, cols]
            vt = vt_ref[0, cols, tok]
            a_row = gt_ref[0, h:h + 1, tok]
            u_row = gt_ref[0, 8 + h:9 + h, tok]
            u_col = gtok_ref[0, tok, MLSTM_GATE_LANE + h:MLSTM_GATE_LANE + h + 1]
            m_prev = m_scr[h:h + 1, 0:1]
            c_prev = c_scr[h]

            g = a_row[:, lm - 1:lm]
            w_end = g + u_row
            m_new = jnp.maximum(g + m_prev, jnp.max(w_end, axis=1, keepdims=True))
            decay = jnp.exp(g + m_prev - m_new)
            wt = jnp.exp(w_end - m_new)

            d_log = jnp.where(causal, a_row + u_col, -jnp.inf)
            inter = a_row + m_prev
            m_t = jnp.maximum(inter, jnp.max(d_log, axis=0, keepdims=True))
            p = jnp.exp(d_log - m_t)
            inter_w = jnp.exp(inter - m_t)

            scores = lax.dot_general(k, q, (((1,), (1,)), ((), ())), preferred_element_type=F32) * p
            cq = lax.dot_general(c_prev.astype(BF16), q, (((1,), (1,)), ((), ())),
                                 preferred_element_type=F32)
            num = jnp.dot(vt, scores.astype(BF16), preferred_element_type=F32) + inter_w * cq[0:dh]
            den = jnp.sum(scores, axis=0, keepdims=True) + inter_w * cq[dh:dh + 1]
            out_t = num / jnp.maximum(jnp.abs(den), jnp.exp(-m_t))
            hn = out_t * lax.rsqrt(jnp.mean(out_t * out_t, axis=0, keepdims=True) + EPS)
            hb = hn.T * gain_ref[:, cols]
            o_ref[0, tok, cols] = (jax.nn.sigmoid(mo_ref[0, tok, cols]) * hb).astype(BF16)

            v_w = jnp.concatenate([vt.astype(F32) * wt, jnp.broadcast_to(wt, (16, lm))], axis=0).astype(BF16)
            c_scr[h] = decay * c_prev + jnp.dot(v_w, k, preferred_element_type=F32)
            m_scr[h:h + 1, :] = jnp.broadcast_to(m_new, (1, LANES))


def _mlstm(mqk, mvt, mo, gtok, gt, gain):
    bsz, seq, _ = mo.shape
    ts, lm = MLSTM_SEQ_TILE, MLSTM_CHUNK
    return pl.pallas_call(
        functools.partial(_mlstm_kernel, ts=ts, lm=lm),
        grid=(bsz, seq // ts),
        in_specs=[pl.BlockSpec((1, ts, MLSTM_WIDTH), lambda b, i: (b, i, 0)),
                  pl.BlockSpec((1, ts, MLSTM_WIDTH), lambda b, i: (b, i, 1)),
                  pl.BlockSpec((1, MLSTM_WIDTH, ts), lambda b, i: (b, 0, i)),
                  pl.BlockSpec((1, ts, MLSTM_WIDTH), lambda b, i: (b, i, 0)),
                  pl.BlockSpec((1, ts, LANES), lambda b, i: (b, i, 0)),
                  pl.BlockSpec((1, 16, ts), lambda b, i: (b, 0, i)),
                  pl.BlockSpec((1, MLSTM_WIDTH), lambda b, i: (0, 0))],
        out_specs=pl.BlockSpec((1, ts, MLSTM_WIDTH), lambda b, i: (b, i, 0)),
        out_shape=jax.ShapeDtypeStruct((bsz, seq, MLSTM_WIDTH), BF16),
        scratch_shapes=[pltpu.VMEM((MLSTM_HEADS, MLSTM_STATE_ROWS, MLSTM_HEAD_DIM), F32),
                        pltpu.VMEM((8, LANES), F32)],
        compiler_params=pltpu.CompilerParams(dimension_semantics=("arbitrary", "arbitrary"),
                                             vmem_limit_bytes=VMEM_LIMIT),
        name="mlstm",
    )(mqk, mqk, mvt, mo, gtok, gt, gain)


def _post_kernel(x1_ref, ya_ref, yb_ref, mod_ref, woa_ref, wob_ref, w2i_ref, w2o_ref, gf_ref, o_ref, x2_scr):
    gt2, sh3 = _rows(mod_ref[0, 5:6, :]), _rows(mod_ref[0, 6:7, :])
    sc3p, gt3h, gain = _rows(1 + mod_ref[0, 7:8, :]), _rows(0.5 * mod_ref[0, 8:9, :]), _rows(gf_ref[...])
    tm = x1_ref.shape[1]
    ya = jnp.dot(ya_ref[0], woa_ref[...], preferred_element_type=F32)
    yb = jnp.dot(yb_ref[0], wob_ref[...], preferred_element_type=F32)

    def residual_and_norm(r):
        x2 = x1_ref[0, r, :] + gt2 * (ya[r] + yb[r])
        x2_scr[r, :] = x2
        return (_rms(x2) * sc3p + sh3).astype(BF16)

    h = _by_rows(residual_and_norm, tm, VEC_ROWS)
    ffn2 = _ffn(h, w2i_ref, w2o_ref)
    for r in range(0, tm, VEC_ROWS):
        x3 = x2_scr[r:r + VEC_ROWS, :] + gt3h * ffn2[r:r + VEC_ROWS]
        o_ref[0, r:r + VEC_ROWS, :] = _rms(x3) * gain


def _post(x1, ya, yb, mod, woa, wob, w2i, w2o, gf):
    bsz, seq, _ = x1.shape
    tm = ROW_TILE
    tok = lambda width: pl.BlockSpec((1, tm, width), lambda b, j: (b, j, 0))
    consts = (woa, wob, w2i, w2o, gf)
    return pl.pallas_call(
        _post_kernel,
        grid=(bsz, seq // tm),
        in_specs=[tok(D_MODEL), tok(FOX_WIDTH), tok(MLSTM_WIDTH),
                  pl.BlockSpec((1, N_MOD, D_MODEL), lambda b, j: (b, 0, 0))]
        + [_resident(a.shape) for a in consts],
        out_specs=tok(D_MODEL),
        out_shape=jax.ShapeDtypeStruct((bsz, seq, D_MODEL), F32),
        scratch_shapes=[pltpu.VMEM((tm, D_MODEL), F32)],
        compiler_params=pltpu.CompilerParams(dimension_semantics=("arbitrary", "arbitrary"),
                                             vmem_limit_bytes=VMEM_LIMIT),
        name="post",
    )(x1, ya, yb, mod, *consts)


def _gate_placement():
    pq = np.zeros((LANES, FOX_HEADS * FOX_GROUP), np.float32)
    pk = np.zeros((LANES, FOX_HEADS * FOX_GROUP), np.float32)
    cq = np.zeros((1, FOX_HEADS * FOX_GROUP), np.float32)
    ck = np.zeros((1, FOX_HEADS * FOX_GROUP), np.float32)
    for h in range(FOX_HEADS):
        base = FOX_GROUP * h + GATE_LANE
        for part in range(3):
            pq[8 * part + h, base + part] = 1.0
            pk[8 * part + h, base + 3 + part] = -1.0
            cq[0, base + 3 + part] = 1.0
            ck[0, base + part] = 1.0
    return jnp.asarray(pq, BF16), jnp.asarray(pk, BF16), jnp.asarray(cq), jnp.asarray(ck)


def kernel(x, c, w_ada, b_ada, w_ffn1_in, w_ffn1_out, w_mix_in, b_fox_f, b_m_i, b_m_f, w_conv, b_conv, g_mhn,
           w_mix_out, w_ffn2_in, w_ffn2_out, g_final):
    assert w_ada.shape[0] == 1, "one layer"
    bsz, seq, _ = x.shape
    assert seq % max(ROW_TILE, MLSTM_SEQ_TILE, FOX_TILE) == 0

    mod = _ada(c, w_ada[0], b_ada[0]).reshape(bsz, N_MOD, D_MODEL)

    w_mix = w_mix_in[0]
    o = 0
    cols = {}
    for name, width in (("fq", FOX_WIDTH), ("fk", FOX_WIDTH), ("fv", FOX_WIDTH), ("ff", FOX_HEADS),
                        ("mq", MLSTM_WIDTH), ("mk", MLSTM_WIDTH), ("mv", MLSTM_WIDTH), ("mo", MLSTM_WIDTH),
                        ("mi", MLSTM_HEADS), ("mf", MLSTM_HEADS)):
        cols[name] = w_mix[:, o:o + width]
        o += width
    wq = (cols["fq"] * (FOX_HEAD_DIM ** -0.5 * LOG2_E)).astype(BF16)
    wk = cols["fk"].astype(BF16)
    wv = cols["fv"].astype(BF16)
    wmqk = jnp.concatenate([cols["mq"], cols["mk"]], axis=1).astype(BF16)
    wmv = cols["mv"].astype(BF16)
    wmo = cols["mo"].astype(BF16)
    zeros = lambda n: jnp.zeros((D_MODEL, n), F32)
    wg = jnp.concatenate([cols["ff"], cols["ff"], cols["ff"], cols["mi"], zeros(LANES - 28),
                          zeros(MLSTM_GATE_LANE), cols["mf"], zeros(LANES - 28)], axis=1).astype(BF16)
    zrow = lambda n: jnp.zeros((n,), F32)
    gbias = jnp.concatenate([b_fox_f[0], b_fox_f[0], b_fox_f[0], b_m_i[0], zrow(LANES - 28),
                             zrow(MLSTM_GATE_LANE), b_m_f[0], zrow(LANES - 28)]).reshape(1, 2 * LANES)
    pq, pk, cq, ck = _gate_placement()
    kscale = jnp.concatenate([jnp.ones((MLSTM_WIDTH,), F32),
                              jnp.full((MLSTM_WIDTH,), MLSTM_HEAD_DIM ** -0.5, F32)]).reshape(1, 2 * MLSTM_WIDTH)

    x1, qa, ka, fvt, mqk, mvt, mo, gtok, gt = _pre(
        x, mod, w_ffn1_in[0].astype(BF16), w_ffn1_out[0].astype(BF16), wq, wk, wv, wmqk, wmv, wmo, wg, gbias,
        pq, pk, cq, ck, w_conv[0], b_conv[0].reshape(1, -1), kscale)

    ya = _fox(qa, ka, fvt)
    yb = _mlstm(mqk, mvt, mo, gtok, gt, g_mhn[0].reshape(1, MLSTM_WIDTH))

    w_out = w_mix_out[0].astype(BF16)
    return _post(x1, ya, yb, mod, w_out[:FOX_WIDTH], w_out[FOX_WIDTH:], w_ffn2_in[0].astype(BF16),
                 w_ffn2_out[0].astype(BF16), g_final.reshape(1, D_MODEL))
```

```python
import functools

import jax
import jax.numpy as jnp
import numpy as np
from jax import lax
from jax.experimental import pallas as pl
from jax.experimental.pallas import tpu as pltpu

F32 = jnp.float32
BF16 = jnp.bfloat16

D_MODEL = 1024
D_FF = 2816
N_MOD = 9
EPS = 1e-6
LOG2_E = 1.4426950408889634
FOX_HEADS = 8
FOX_HEAD_DIM = 64
FOX_WIDTH = FOX_HEADS * FOX_HEAD_DIM
MLSTM_HEADS = 4
MLSTM_HEAD_DIM = 128
MLSTM_WIDTH = MLSTM_HEADS * MLSTM_HEAD_DIM
CONV_WIDTH = 4

LANES = 128
VEC_ROWS = 16
FOX_GROUP = 128
GATE_LANE = FOX_HEAD_DIM
MLSTM_GATE_LANE = 24
MLSTM_STATE_ROWS = MLSTM_HEAD_DIM + 16

ROW_TILE = 512
FFN_CHUNKS = (512, 512, 512, 512, 512, 256)
MLSTM_CHUNK = 256
MLSTM_SEQ_TILE = 512
FOX_TILE = 512
FOX_STEP_HEADS = 8
FOX_ISSUE_HEADS = 2
FOX_ROW_BLOCK = 32
VMEM_LIMIT = 56 * 1024 * 1024


def _rms(x):
    return x * lax.rsqrt(jnp.mean(x * x, axis=-1, keepdims=True) + EPS)


def _by_rows(fn, rows, step):
    return jnp.concatenate([fn(slice(r, r + step)) for r in range(0, rows, step)], axis=0)


def _rows(row):
    return jnp.broadcast_to(row, (VEC_ROWS, row.shape[-1]))


def _split3(x):
    hi = x.astype(BF16)
    r1 = x - hi.astype(F32)
    mid = r1.astype(BF16)
    lo = (r1 - mid.astype(F32)).astype(BF16)
    return hi, mid, lo


def _ffn(h, w_in_ref, w_out_ref):
    acc = None
    j0 = 0
    for width in FFN_CHUNKS:
        g = jnp.dot(h, w_in_ref[:, j0:j0 + width], preferred_element_type=F32)
        u = jnp.dot(h, w_in_ref[:, D_FF + j0:D_FF + j0 + width], preferred_element_type=F32)
        a = _by_rows(lambda r: (jax.nn.silu(g[r]) * u[r]).astype(BF16), g.shape[0], 2 * VEC_ROWS)
        part = jnp.dot(a, w_out_ref[j0:j0 + width, :], preferred_element_type=F32)
        acc = part if acc is None else acc + part
        j0 += width
    return acc


def _resident(shape):
    zeros = (0,) * len(shape)
    return pl.BlockSpec(shape, lambda *_: zeros, pipeline_mode=pl.Buffered(1))


def _ada_kernel(c_ref, w_ref, b_ref, o_ref):
    c_act = jax.nn.silu(c_ref[...])
    o_ref[...] = jnp.dot(c_act, w_ref[...], preferred_element_type=F32,
                         precision=lax.Precision.HIGHEST) + b_ref[...]


def _ada(c, w, b):
    bsz = c.shape[0]
    rows = 8
    c_pad = jnp.zeros((rows, D_MODEL), F32).at[:bsz].set(c)
    n = w.shape[1]
    tn = 1536
    out = pl.pallas_call(
        _ada_kernel,
        grid=(n // tn,),
        in_specs=[pl.BlockSpec((rows, D_MODEL), lambda j: (0, 0)),
                  pl.BlockSpec((D_MODEL, tn), lambda j: (0, j)),
                  pl.BlockSpec((1, tn), lambda j: (0, j))],
        out_specs=pl.BlockSpec((rows, tn), lambda j: (0, j)),
        out_shape=jax.ShapeDtypeStruct((rows, n), F32),
        compiler_params=pltpu.CompilerParams(dimension_semantics=("arbitrary",),
                                             vmem_limit_bytes=VMEM_LIMIT),
        name="ada",
    )(c_pad, w, b.reshape(1, n))
    return out[:bsz]


def _pre_kernel(x_ref, mod_ref, w1i_ref, w1o_ref, wq_ref, wk_ref, wv_ref, wmqk_ref, wmv_ref, wmo_ref,
                wg_ref, gbias_ref, pq_ref, pk_ref, cq_ref, ck_ref, convw_ref, convb_ref, kscale_ref,
                x1_ref, qa_ref, ka_ref, fvt_ref, mqk_ref, mvt_ref, mo_ref, gtok_ref, gt_ref,
                cum_carry, conv_tail, *, tm, lm):
    hm = tm // 2
    sh1, sc1p, gt1h = _rows(mod_ref[0, 0:1, :]), _rows(1 + mod_ref[0, 1:2, :]), _rows(0.5 * mod_ref[0, 2:3, :])
    sh2, sc2p = _rows(mod_ref[0, 3:4, :]), _rows(1 + mod_ref[0, 4:5, :])

    @pl.when(pl.program_id(1) == 0)
    def _():
        cum_carry[...] = jnp.zeros_like(cum_carry)
        conv_tail[...] = jnp.zeros_like(conv_tail)

    row = lax.broadcasted_iota(jnp.int32, (hm, hm), 0)
    col = lax.broadcasted_iota(jnp.int32, (hm, hm), 1)
    dist = row - col
    tri_seq = jnp.where(dist >= 0, 1.0, 0.0).astype(BF16)
    tri_chunk = jnp.where(dist >= 0, jnp.where(dist <= (row & (lm - 1)), 1.0, 0.0), 0.0).astype(BF16)
    lane = lax.broadcasted_iota(jnp.int32, (hm, LANES), 1)
    head_lane = lax.broadcasted_iota(jnp.int32, (VEC_ROWS, FOX_GROUP), 1) < FOX_HEAD_DIM
    q_ones, k_ones = _rows(cq_ref[...]), _rows(ck_ref[...])

    def cumsum_rows(tri, v):
        hi, mid, lo = _split3(v)
        r = jnp.dot(tri, jnp.concatenate([hi, mid, lo], axis=1), preferred_element_type=F32)
        return (r[:, :LANES] + r[:, LANES:2 * LANES]) + r[:, 2 * LANES:]

    def fox_groups(proj, gate, ones, r):
        groups = []
        for pair in range(FOX_HEADS // 2):
            both = proj[r, FOX_GROUP * pair:FOX_GROUP * (pair + 1)]
            for head, vals in ((2 * pair, both), (2 * pair + 1, pltpu.roll(both, FOX_HEAD_DIM, axis=1))):
                lanes = slice(FOX_GROUP * head, FOX_GROUP * (head + 1))
                groups.append(jnp.where(head_lane, vals, gate[r, lanes] + ones[:, lanes]))
        return jnp.concatenate(groups, axis=1).astype(BF16)

    def norm1(base):
        return _by_rows(lambda r: (_rms(x_ref[0, base + r.start:base + r.stop, :]) * sc1p + sh1).astype(BF16),
                        hm, VEC_ROWS)

    def residual_and_norm(base, ffn):
        def block(r):
            rows = slice(base + r.start, base + r.stop)
            x1 = x_ref[0, rows, :] + gt1h * ffn[r]
            x1_ref[0, rows, :] = x1
            return (_rms(x1) * sc2p + sh2).astype(BF16)
        return _by_rows(block, hm, VEC_ROWS)

    def mix(base, h2, cum_in, tail):
        rows = slice(base, base + hm)
        z = jnp.dot(h2, wg_ref[...], preferred_element_type=F32) + gbias_ref[...]
        mqk = jnp.dot(h2, wmqk_ref[...], preferred_element_type=F32)
        za, zb = z[:, :LANES], z[:, LANES:]
        fvt_ref[0, :, rows] = jnp.dot(h2, wv_ref[...], preferred_element_type=F32).T.astype(BF16)

        cum_f = cumsum_rows(tri_seq, jax.nn.log_sigmoid(za)) + cum_in
        a_loc = cumsum_rows(tri_chunk, jax.nn.log_sigmoid(zb))
        u_gate = za - a_loc
        gtok_ref[0, rows, :] = u_gate
        gt_ref[0, 0:8, rows] = a_loc.T[MLSTM_GATE_LANE:MLSTM_GATE_LANE + 8, :]
        gt_ref[0, 8:16, rows] = u_gate.T[MLSTM_GATE_LANE:MLSTM_GATE_LANE + 8, :]
        cum_2 = cum_f * LOG2_E
        c_hi = cum_2.astype(BF16).astype(F32)
        c_mid = (cum_2 - c_hi).astype(BF16).astype(F32)
        c_lo = (cum_2 - c_hi) - c_mid
        packed = jnp.where(lane < 8, c_hi, jnp.where(lane < 16, c_mid, jnp.where(lane < 24, c_lo, 0.0))).astype(BF16)

        mvt_ref[0, :, rows] = jnp.dot(h2, wmv_ref[...], preferred_element_type=F32).T.astype(BF16)

        ext = jnp.concatenate([tail, mqk], axis=0)
        for c in range(0, 2 * MLSTM_WIDTH, MLSTM_WIDTH):
            cols = slice(c, c + MLSTM_WIDTH)
            taps = [_rows(convw_ref[t:t + 1, cols]) for t in range(CONV_WIDTH)]
            bias, out_scale = _rows(convb_ref[:, cols]), _rows(kscale_ref[:, cols])
            for r in range(0, hm, VEC_ROWS):
                y = bias + taps[3] * ext[8 + r:8 + r + VEC_ROWS, cols]
                y = y + taps[2] * ext[7 + r:7 + r + VEC_ROWS, cols]
                y = y + taps[1] * ext[6 + r:6 + r + VEC_ROWS, cols]
                y = y + taps[0] * ext[5 + r:5 + r + VEC_ROWS, cols]
                mqk_ref[0, base + r:base + r + VEC_ROWS, cols] = (jax.nn.silu(y) * out_scale).astype(BF16)

        mo_ref[0, rows, :] = jnp.dot(h2, wmo_ref[...], preferred_element_type=F32)
        q = jnp.dot(h2, wq_ref[...], preferred_element_type=F32)
        q_gate = jnp.dot(packed, pq_ref[...], preferred_element_type=F32)
        for r in range(0, hm, VEC_ROWS):
            qa_ref[0, base + r:base + r + VEC_ROWS, :] = fox_groups(q, q_gate, q_ones, slice(r, r + VEC_ROWS))
        k = jnp.dot(h2, wk_ref[...], preferred_element_type=F32)
        k_gate = jnp.dot(packed, pk_ref[...], preferred_element_type=F32)
        for r in range(0, hm, VEC_ROWS):
            ka_ref[0, base + r:base + r + VEC_ROWS, :] = fox_groups(k, k_gate, k_ones, slice(r, r + VEC_ROWS))
        return cum_f[hm - 1:hm, :], mqk[hm - 8:hm, :]

    h_a, h_b = norm1(0), norm1(hm)
    h2_a = residual_and_norm(0, _ffn(h_a, w1i_ref, w1o_ref))
    ffn_b = _ffn(h_b, w1i_ref, w1o_ref)
    cum_a, tail_a = mix(0, h2_a, cum_carry[0:1, :], conv_tail[...])
    h2_b = residual_and_norm(hm, ffn_b)
    cum_b, tail_b = mix(hm, h2_b, cum_a, tail_a)
    cum_carry[0:1, :] = cum_b
    conv_tail[...] = tail_b


def _pre(x, mod, w1i, w1o, wq, wk, wv, wmqk, wmv, wmo, wg, gbias, pq, pk, cq, ck, convw, convb, kscale):
    bsz, seq, _ = x.shape
    tm = ROW_TILE
    grid = (bsz, seq // tm)
    tok = lambda width: pl.BlockSpec((1, tm, width), lambda b, j: (b, j, 0))
    feat = lambda rows: pl.BlockSpec((1, rows, tm), lambda b, j: (b, 0, j))
    consts = (w1i, w1o, wq, wk, wv, wmqk, wmv, wmo, wg, gbias, pq, pk, cq, ck, convw, convb, kscale)
    out_shape = (
        jax.ShapeDtypeStruct((bsz, seq, D_MODEL), F32),
        jax.ShapeDtypeStruct((bsz, seq, FOX_HEADS * FOX_GROUP), BF16),
        jax.ShapeDtypeStruct((bsz, seq, FOX_HEADS * FOX_GROUP), BF16),
        jax.ShapeDtypeStruct((bsz, FOX_WIDTH, seq), BF16),
        jax.ShapeDtypeStruct((bsz, seq, 2 * MLSTM_WIDTH), BF16),
        jax.ShapeDtypeStruct((bsz, MLSTM_WIDTH, seq), BF16),
        jax.ShapeDtypeStruct((bsz, seq, MLSTM_WIDTH), F32),
        jax.ShapeDtypeStruct((bsz, seq, LANES), F32),
        jax.ShapeDtypeStruct((bsz, 16, seq), F32),
    )
    out_specs = (tok(D_MODEL), tok(FOX_HEADS * FOX_GROUP), tok(FOX_HEADS * FOX_GROUP), feat(FOX_WIDTH),
                 tok(2 * MLSTM_WIDTH), feat(MLSTM_WIDTH), tok(MLSTM_WIDTH), tok(LANES), feat(16))
    return pl.pallas_call(
        functools.partial(_pre_kernel, tm=tm, lm=MLSTM_CHUNK),
        grid=grid,
        in_specs=[tok(D_MODEL), pl.BlockSpec((1, N_MOD, D_MODEL), lambda b, j: (b, 0, 0))]
        + [_resident(a.shape) for a in consts],
        out_specs=out_specs,
        out_shape=out_shape,
        scratch_shapes=[pltpu.VMEM((8, LANES), F32), pltpu.VMEM((8, 2 * MLSTM_WIDTH), F32)],
        compiler_params=pltpu.CompilerParams(dimension_semantics=("arbitrary", "arbitrary"),
                                             vmem_limit_bytes=VMEM_LIMIT),
        name="pre",
    )(x, mod, *consts)


def _fox_kernel(q_ref, k_ref, vt_ref, o_ref, s_scr, p_scr, m_scr, alpha_scr, acc_scr, *, tq):
    qi = pl.program_id(2)
    heads = range(FOX_STEP_HEADS)
    lanes = [slice(FOX_GROUP * h, FOX_GROUP * (h + 1)) for h in heads]
    rows = [slice(FOX_HEAD_DIM * h, FOX_HEAD_DIM * (h + 1)) for h in heads]

    def put_scores(h, k0):
        s_scr[h] = lax.dot_general(k_ref[0, pl.ds(k0, tq), lanes[h]], q_ref[0, :, lanes[h]],
                                   (((1,), (1,)), ((), ())), preferred_element_type=F32)

    def softmax(h, masked):
        def block(r):
            s = s_scr[h, r:r + FOX_ROW_BLOCK, :]
            if masked:
                kpos = r + lax.broadcasted_iota(jnp.int32, (FOX_ROW_BLOCK, tq), 0)
                qpos = lax.broadcasted_iota(jnp.int32, (FOX_ROW_BLOCK, tq), 1)
                s = jnp.where(kpos <= qpos, s, -jnp.inf)
            return s

        top = block(0)
        for r in range(FOX_ROW_BLOCK, tq, FOX_ROW_BLOCK):
            top = jnp.maximum(top, block(r))
        m = m_scr[h, 0:1, :]
        m_new = jnp.maximum(m, jnp.max(top, axis=0, keepdims=True))
        alpha_scr[h, 0:1, :] = jnp.exp2(m - m_new)
        m_scr[h, 0:1, :] = m_new
        for r in range(0, tq, FOX_ROW_BLOCK):
            p_scr[h, r:r + FOX_ROW_BLOCK, :] = jnp.exp2(block(r) - m_new).astype(BF16)

    def weighted_values(h, k0):
        v_aug = jnp.concatenate([vt_ref[0, rows[h], pl.ds(k0, tq)], jnp.ones((16, tq), BF16)], axis=0)
        acc_scr[h] = alpha_scr[h, 0:1, :] * acc_scr[h] + jnp.dot(v_aug, p_scr[h], preferred_element_type=F32)

    def step(t, masked):
        k0 = pl.multiple_of(t * tq, tq)
        k_prev = pl.multiple_of(jnp.maximum(k0 - tq, 0), tq)
        for g in range(0, FOX_STEP_HEADS, FOX_ISSUE_HEADS):
            group = range(g, g + FOX_ISSUE_HEADS)
            for h in group:
                put_scores(h, k0)
            for h in group:
                weighted_values(h, k_prev)
            for h in group:
                softmax(h, masked)

    m_scr[...] = jnp.full(m_scr.shape, -jnp.inf, F32)
    alpha_scr[...] = jnp.ones(alpha_scr.shape, F32)
    acc_scr[...] = jnp.zeros(acc_scr.shape, F32)
    p_scr[...] = jnp.zeros(p_scr.shape, BF16)

    @pl.loop(0, qi)
    def _(t):
        step(t, False)

    step(qi, True)
    outs = []
    for h in heads:
        weighted_values(h, pl.multiple_of(qi * tq, tq))
        a = acc_scr[h]
        outs.append(a[0:FOX_HEAD_DIM] / a[FOX_HEAD_DIM:FOX_HEAD_DIM + 1])
    o_ref[0] = jnp.concatenate(outs, axis=0).T.astype(BF16)


def _fox(qa, ka, fvt):
    bsz, seq, _ = qa.shape
    tq, nh = FOX_TILE, FOX_STEP_HEADS
    return pl.pallas_call(
        functools.partial(_fox_kernel, tq=tq),
        grid=(bsz, FOX_HEADS // nh, seq // tq),
        in_specs=[pl.BlockSpec((1, tq, nh * FOX_GROUP), lambda b, g, i: (b, i, g)),
                  pl.BlockSpec((1, seq, nh * FOX_GROUP), lambda b, g, i: (b, 0, g)),
                  pl.BlockSpec((1, nh * FOX_HEAD_DIM, seq), lambda b, g, i: (b, g, 0))],
        out_specs=pl.BlockSpec((1, tq, nh * FOX_HEAD_DIM), lambda b, g, i: (b, i, g)),
        out_shape=jax.ShapeDtypeStruct((bsz, seq, FOX_WIDTH), BF16),
        scratch_shapes=[pltpu.VMEM((nh, tq, tq), F32), pltpu.VMEM((nh, tq, tq), BF16),
                        pltpu.VMEM((nh, 8, tq), F32), pltpu.VMEM((nh, 8, tq), F32),
                        pltpu.VMEM((nh, FOX_HEAD_DIM + 16, tq), F32)],
        compiler_params=pltpu.CompilerParams(dimension_semantics=("arbitrary",) * 3,
                                             vmem_limit_bytes=VMEM_LIMIT),
        name="fox",
    )(qa, ka, fvt)


def _mlstm_kernel(q_ref, k_ref, vt_ref, mo_ref, gtok_ref, gt_ref, gain_ref, o_ref, c_scr, m_scr, *, ts, lm):
    @pl.when(pl.program_id(1) == 0)
    def _():
        c_scr[...] = jnp.zeros_like(c_scr)
        m_scr[...] = jnp.zeros_like(m_scr)

    s_idx = lax.broadcasted_iota(jnp.int32, (lm, lm), 0)
    t_idx = lax.broadcasted_iota(jnp.int32, (lm, lm), 1)
    causal = s_idx <= t_idx
    dh = MLSTM_HEAD_DIM
    for c in range(ts // lm):
        tok = slice(c * lm, (c + 1) * lm)
        for h in range(MLSTM_HEADS):
            cols = slice(dh * h, dh * (h + 1))
            q = q_ref[0, tok, cols]
            k = k_ref[0, tok, cols]
            vt = vt_ref[0, cols, tok]
            a_row = gt_ref[0, h:h + 1, tok]
            u_row = gt_ref[0, 8 + h:9 + h, tok]
            u_col = gtok_ref[0, tok, MLSTM_GATE_LANE + h:MLSTM_GATE_LANE + h + 1]
            m_prev = m_scr[h:h + 1, 0:1]
            c_prev = c_scr[h]

            g = a_row[:, lm - 1:lm]
            w_end = g + u_row
            m_new = jnp.maximum(g + m_prev, jnp.max(w_end, axis=1, keepdims=True))
            decay = jnp.exp(g + m_prev - m_new)
            wt = jnp.exp(w_end - m_new)

            d_log = jnp.where(causal, a_row + u_col, -jnp.inf)
            inter = a_row + m_prev
            m_t = jnp.maximum(inter, jnp.max(d_log, axis=0, keepdims=True))
            p = jnp.exp(d_log - m_t)
            inter_w = jnp.exp(inter - m_t)

            scores = lax.dot_general(k, q, (((1,), (1,)), ((), ())), preferred_element_type=F32) * p
            cq = lax.dot_general(c_prev.astype(BF16), q, (((1,), (1,)), ((), ())),
                                 preferred_element_type=F32)
            num = jnp.dot(vt, scores.astype(BF16), preferred_element_type=F32) + inter_w * cq[0:dh]
            den = jnp.sum(scores, axis=0, keepdims=True) + inter_w * cq[dh:dh + 1]
            out_t = num / jnp.maximum(jnp.abs(den), jnp.exp(-m_t))
            hn = out_t * lax.rsqrt(jnp.mean(out_t * out_t, axis=0, keepdims=True) + EPS)
            hb = hn.T * gain_ref[:, cols]
            o_ref[0, tok, cols] = (jax.nn.sigmoid(mo_ref[0, tok, cols]) * hb).astype(BF16)

            v_w = jnp.concatenate([vt.astype(F32) * wt, jnp.broadcast_to(wt, (16, lm))], axis=0).astype(BF16)
            c_scr[h] = decay * c_prev + jnp.dot(v_w, k, preferred_element_type=F32)
            m_scr[h:h + 1, :] = jnp.broadcast_to(m_new, (1, LANES))


def _mlstm(mqk, mvt, mo, gtok, gt, gain):
    bsz, seq, _ = mo.shape
    ts, lm = MLSTM_SEQ_TILE, MLSTM_CHUNK
    return pl.pallas_call(
        functools.partial(_mlstm_kernel, ts=ts, lm=lm),
        grid=(bsz, seq // ts),
        in_specs=[pl.BlockSpec((1, ts, MLSTM_WIDTH), lambda b, i: (b, i, 0)),
                  pl.BlockSpec((1, ts, MLSTM_WIDTH), lambda b, i: (b, i, 1)),
                  pl.BlockSpec((1, MLSTM_WIDTH, ts), lambda b, i: (b, 0, i)),
                  pl.BlockSpec((1, ts, MLSTM_WIDTH), lambda b, i: (b, i, 0)),
                  pl.BlockSpec((1, ts, LANES), lambda b, i: (b, i, 0)),
                  pl.BlockSpec((1, 16, ts), lambda b, i: (b, 0, i)),
                  pl.BlockSpec((1, MLSTM_WIDTH), lambda b, i: (0, 0))],
        out_specs=pl.BlockSpec((1, ts, MLSTM_WIDTH), lambda b, i: (b, i, 0)),
        out_shape=jax.ShapeDtypeStruct((bsz, seq, MLSTM_WIDTH), BF16),
        scratch_shapes=[pltpu.VMEM((MLSTM_HEADS, MLSTM_STATE_ROWS, MLSTM_HEAD_DIM), F32),
                        pltpu.VMEM((8, LANES), F32)],
        compiler_params=pltpu.CompilerParams(dimension_semantics=("arbitrary", "arbitrary"),
                                             vmem_limit_bytes=VMEM_LIMIT),
        name="mlstm",
    )(mqk, mqk, mvt, mo, gtok, gt, gain)


def _post_kernel(x1_ref, ya_ref, yb_ref, mod_ref, woa_ref, wob_ref, w2i_ref, w2o_ref, gf_ref, o_ref, x2_scr):
    gt2, sh3 = _rows(mod_ref[0, 5:6, :]), _rows(mod_ref[0, 6:7, :])
    sc3p, gt3h, gain = _rows(1 + mod_ref[0, 7:8, :]), _rows(0.5 * mod_ref[0, 8:9, :]), _rows(gf_ref[...])
    tm = x1_ref.shape[1]
    ya = jnp.dot(ya_ref[0], woa_ref[...], preferred_element_type=F32)
    yb = jnp.dot(yb_ref[0], wob_ref[...], preferred_element_type=F32)

    def residual_and_norm(r):
        x2 = x1_ref[0, r, :] + gt2 * (ya[r] + yb[r])
        x2_scr[r, :] = x2
        return (_rms(x2) * sc3p + sh3).astype(BF16)

    h = _by_rows(residual_and_norm, tm, VEC_ROWS)
    ffn2 = _ffn(h, w2i_ref, w2o_ref)
    for r in range(0, tm, VEC_ROWS):
        x3 = x2_scr[r:r + VEC_ROWS, :] + gt3h * ffn2[r:r + VEC_ROWS]
        o_ref[0, r:r + VEC_ROWS, :] = _rms(x3) * gain


def _post(x1, ya, yb, mod, woa, wob, w2i, w2o, gf):
    bsz, seq, _ = x1.shape
    tm = ROW_TILE
    tok = lambda width: pl.BlockSpec((1, tm, width), lambda b, j: (b, j, 0))
    consts = (woa, wob, w2i, w2o, gf)
    return pl.pallas_call(
        _post_kernel,
        grid=(bsz, seq // tm),
        in_specs=[tok(D_MODEL), tok(FOX_WIDTH), tok(MLSTM_WIDTH),
                  pl.BlockSpec((1, N_MOD, D_MODEL), lambda b, j: (b, 0, 0))]
        + [_resident(a.shape) for a in consts],
        out_specs=tok(D_MODEL),
        out_shape=jax.ShapeDtypeStruct((bsz, seq, D_MODEL), F32),
        scratch_shapes=[pltpu.VMEM((tm, D_MODEL), F32)],
        compiler_params=pltpu.CompilerParams(dimension_semantics=("arbitrary", "arbitrary"),
                                             vmem_limit_bytes=VMEM_LIMIT),
        name="post",
    )(x1, ya, yb, mod, *consts)


def _gate_placement():
    pq = np.zeros((LANES, FOX_HEADS * FOX_GROUP), np.float32)
    pk = np.zeros((LANES, FOX_HEADS * FOX_GROUP), np.float32)
    cq = np.zeros((1, FOX_HEADS * FOX_GROUP), np.float32)
    ck = np.zeros((1, FOX_HEADS * FOX_GROUP), np.float32)
    for h in range(FOX_HEADS):
        base = FOX_GROUP * h + GATE_LANE
        for part in range(3):
            pq[8 * part + h, base + part] = 1.0
            pk[8 * part + h, base + 3 + part] = -1.0
            cq[0, base + 3 + part] = 1.0
            ck[0, base + part] = 1.0
    return jnp.asarray(pq, BF16), jnp.asarray(pk, BF16), jnp.asarray(cq), jnp.asarray(ck)


def kernel(x, c, w_ada, b_ada, w_ffn1_in, w_ffn1_out, w_mix_in, b_fox_f, b_m_i, b_m_f, w_conv, b_conv, g_mhn,
           w_mix_out, w_ffn2_in, w_ffn2_out, g_final):
    assert w_ada.shape[0] == 1, "one layer"
    bsz, seq, _ = x.shape
    assert seq % max(ROW_TILE, MLSTM_SEQ_TILE, FOX_TILE) == 0

    mod = _ada(c, w_ada[0], b_ada[0]).reshape(bsz, N_MOD, D_MODEL)

    w_mix = w_mix_in[0]
    o = 0
    cols = {}
    for name, width in (("fq", FOX_WIDTH), ("fk", FOX_WIDTH), ("fv", FOX_WIDTH), ("ff", FOX_HEADS),
                        ("mq", MLSTM_WIDTH), ("mk", MLSTM_WIDTH), ("mv", MLSTM_WIDTH), ("mo", MLSTM_WIDTH),
                        ("mi", MLSTM_HEADS), ("mf", MLSTM_HEADS)):
        cols[name] = w_mix[:, o:o + width]
        o += width
    wq = (cols["fq"] * (FOX_HEAD_DIM ** -0.5 * LOG2_E)).astype(BF16)
    wk = cols["fk"].astype(BF16)
    wv = cols["fv"].astype(BF16)
    wmqk = jnp.concatenate([cols["mq"], cols["mk"]], axis=1).astype(BF16)
    wmv = cols["mv"].astype(BF16)
    wmo = cols["mo"].astype(BF16)
    zeros = lambda n: jnp.zeros((D_MODEL, n), F32)
    wg = jnp.concatenate([cols["ff"], cols["ff"], cols["ff"], cols["mi"], zeros(LANES - 28),
                          zeros(MLSTM_GATE_LANE), cols["mf"], zeros(LANES - 28)], axis=1).astype(BF16)
    zrow = lambda n: jnp.zeros((n,), F32)
    gbias = jnp.concatenate([b_fox_f[0], b_fox_f[0], b_fox_f[0], b_m_i[0], zrow(LANES - 28),
                             zrow(MLSTM_GATE_LANE), b_m_f[0], zrow(LANES - 28)]).reshape(1, 2 * LANES)
    pq, pk, cq, ck = _gate_placement()
    kscale = jnp.concatenate([jnp.ones((MLSTM_WIDTH,), F32),
                              jnp.full((MLSTM_WIDTH,), MLSTM_HEAD_DIM ** -0.5, F32)]).reshape(1, 2 * MLSTM_WIDTH)

    x1, qa, ka, fvt, mqk, mvt, mo, gtok, gt = _pre(
        x, mod, w_ffn1_in[0].astype(BF16), w_ffn1_out[0].astype(BF16), wq, wk, wv, wmqk, wmv, wmo, wg, gbias,
        pq, pk, cq, ck, w_conv[0], b_conv[0].reshape(1, -1), kscale)

    ya = _fox(qa, ka, fvt)
    yb = _mlstm(mqk, mvt, mo, gtok, gt, g_mhn[0].reshape(1, MLSTM_WIDTH))

    w_out = w_mix_out[0].astype(BF16)
    return _post(x1, ya, yb, mod, w_out[:FOX_WIDTH], w_out[FOX_WIDTH:], w_ffn2_in[0].astype(BF16),
                 w_ffn2_out[0].astype(BF16), g_final.reshape(1, D_MODEL))
```

```python
import functools

import jax
import jax.numpy as jnp
import numpy as np
from jax import lax
from jax.experimental import pallas as pl
from jax.experimental.pallas import tpu as pltpu

F32 = jnp.float32
BF16 = jnp.bfloat16

D_MODEL = 1024
D_FF = 2816
N_MOD = 9
EPS = 1e-6
LOG2_E = 1.4426950408889634
FOX_HEADS = 8
FOX_HEAD_DIM = 64
FOX_WIDTH = FOX_HEADS * FOX_HEAD_DIM
MLSTM_HEADS = 4
MLSTM_HEAD_DIM = 128
MLSTM_WIDTH = MLSTM_HEADS * MLSTM_HEAD_DIM
CONV_WIDTH = 4

LANES = 128
VEC_ROWS = 16
FOX_GROUP = 128
GATE_LANE = FOX_HEAD_DIM
MLSTM_GATE_LANE = 24
MLSTM_STATE_ROWS = MLSTM_HEAD_DIM + 16

ROW_TILE = 512
FFN_CHUNKS = (512, 512, 512, 512, 512, 256)
MLSTM_CHUNK = 256
MLSTM_SEQ_TILE = 1024
FOX_TILE = 512
FOX_STEP_HEADS = 8
FOX_ISSUE_HEADS = 2
FOX_ROW_BLOCK = 32
VMEM_LIMIT = 56 * 1024 * 1024


def _rms(x):
    return x * lax.rsqrt(jnp.mean(x * x, axis=-1, keepdims=True) + EPS)


def _by_rows(fn, rows, step):
    return jnp.concatenate([fn(slice(r, r + step)) for r in range(0, rows, step)], axis=0)


def _rows(row):
    return jnp.broadcast_to(row, (VEC_ROWS, row.shape[-1]))


def _split3(x):
    hi = x.astype(BF16)
    r1 = x - hi.astype(F32)
    mid = r1.astype(BF16)
    lo = (r1 - mid.astype(F32)).astype(BF16)
    return hi, mid, lo


def _ffn(h, w_in_ref, w_out_ref):
    acc = None
    j0 = 0
    for width in FFN_CHUNKS:
        g = jnp.dot(h, w_in_ref[:, j0:j0 + width], preferred_element_type=F32)
        u = jnp.dot(h, w_in_ref[:, D_FF + j0:D_FF + j0 + width], preferred_element_type=F32)
        a = _by_rows(lambda r: (jax.nn.silu(g[r]) * u[r]).astype(BF16), g.shape[0], 2 * VEC_ROWS)
        part = jnp.dot(a, w_out_ref[j0:j0 + width, :], preferred_element_type=F32)
        acc = part if acc is None else acc + part
        j0 += width
    return acc


def _resident(shape):
    zeros = (0,) * len(shape)
    return pl.BlockSpec(shape, lambda *_: zeros, pipeline_mode=pl.Buffered(1))


def _ada_kernel(c_ref, w_ref, b_ref, o_ref):
    c_act = jax.nn.silu(c_ref[...])
    o_ref[...] = jnp.dot(c_act, w_ref[...], preferred_element_type=F32,
                         precision=lax.Precision.HIGHEST) + b_ref[...]


def _ada(c, w, b):
    bsz = c.shape[0]
    rows = 8
    c_pad = jnp.zeros((rows, D_MODEL), F32).at[:bsz].set(c)
    n = w.shape[1]
    tn = 1536
    out = pl.pallas_call(
        _ada_kernel,
        grid=(n // tn,),
        in_specs=[pl.BlockSpec((rows, D_MODEL), lambda j: (0, 0)),
                  pl.BlockSpec((D_MODEL, tn), lambda j: (0, j)),
                  pl.BlockSpec((1, tn), lambda j: (0, j))],
        out_specs=pl.BlockSpec((rows, tn), lambda j: (0, j)),
        out_shape=jax.ShapeDtypeStruct((rows, n), F32),
        compiler_params=pltpu.CompilerParams(dimension_semantics=("arbitrary",),
                                             vmem_limit_bytes=VMEM_LIMIT),
        name="ada",
    )(c_pad, w, b.reshape(1, n))
    return out[:bsz]


def _pre_kernel(x_ref, mod_ref, w1i_ref, w1o_ref, wq_ref, wk_ref, wv_ref, wmqk_ref, wmv_ref, wmo_ref,
                wg_ref, gbias_ref, pq_ref, pk_ref, cq_ref, ck_ref, convw_ref, convb_ref, kscale_ref,
                x1_ref, qa_ref, ka_ref, fvt_ref, mqk_ref, mvt_ref, mo_ref, gtok_ref, gt_ref,
                cum_carry, conv_tail, *, tm, lm):
    hm = tm // 2
    sh1, sc1p, gt1h = _rows(mod_ref[0, 0:1, :]), _rows(1 + mod_ref[0, 1:2, :]), _rows(0.5 * mod_ref[0, 2:3, :])
    sh2, sc2p = _rows(mod_ref[0, 3:4, :]), _rows(1 + mod_ref[0, 4:5, :])

    @pl.when(pl.program_id(1) == 0)
    def _():
        cum_carry[...] = jnp.zeros_like(cum_carry)
        conv_tail[...] = jnp.zeros_like(conv_tail)

    row = lax.broadcasted_iota(jnp.int32, (hm, hm), 0)
    col = lax.broadcasted_iota(jnp.int32, (hm, hm), 1)
    dist = row - col
    tri_seq = jnp.where(dist >= 0, 1.0, 0.0).astype(BF16)
    tri_chunk = jnp.where(dist >= 0, jnp.where(dist <= (row & (lm - 1)), 1.0, 0.0), 0.0).astype(BF16)
    lane = lax.broadcasted_iota(jnp.int32, (hm, LANES), 1)
    head_lane = lax.broadcasted_iota(jnp.int32, (VEC_ROWS, FOX_GROUP), 1) < FOX_HEAD_DIM
    q_ones, k_ones = _rows(cq_ref[...]), _rows(ck_ref[...])

    def cumsum_rows(tri, v):
        hi, mid, lo = _split3(v)
        r = jnp.dot(tri, jnp.concatenate([hi, mid, lo], axis=1), preferred_element_type=F32)
        return (r[:, :LANES] + r[:, LANES:2 * LANES]) + r[:, 2 * LANES:]

    def fox_groups(proj, gate, ones, r):
        groups = []
        for pair in range(FOX_HEADS // 2):
            both = proj[r, FOX_GROUP * pair:FOX_GROUP * (pair + 1)]
            for head, vals in ((2 * pair, both), (2 * pair + 1, pltpu.roll(both, FOX_HEAD_DIM, axis=1))):
                lanes = slice(FOX_GROUP * head, FOX_GROUP * (head + 1))
                groups.append(jnp.where(head_lane, vals, gate[r, lanes] + ones[:, lanes]))
        return jnp.concatenate(groups, axis=1).astype(BF16)

    def norm1(base):
        return _by_rows(lambda r: (_rms(x_ref[0, base + r.start:base + r.stop, :]) * sc1p + sh1).astype(BF16),
                        hm, VEC_ROWS)

    def residual_and_norm(base, ffn):
        def block(r):
            rows = slice(base + r.start, base + r.stop)
            x1 = x_ref[0, rows, :] + gt1h * ffn[r]
            x1_ref[0, rows, :] = x1
            return (_rms(x1) * sc2p + sh2).astype(BF16)
        return _by_rows(block, hm, VEC_ROWS)

    def mix(base, h2, cum_in, tail):
        rows = slice(base, base + hm)
        z = jnp.dot(h2, wg_ref[...], preferred_element_type=F32) + gbias_ref[...]
        mqk = jnp.dot(h2, wmqk_ref[...], preferred_element_type=F32)
        za, zb = z[:, :LANES], z[:, LANES:]
        fvt_ref[0, :, rows] = jnp.dot(h2, wv_ref[...], preferred_element_type=F32).T.astype(BF16)

        cum_f = cumsum_rows(tri_seq, jax.nn.log_sigmoid(za)) + cum_in
        a_loc = cumsum_rows(tri_chunk, jax.nn.log_sigmoid(zb))
        u_gate = za - a_loc
        gtok_ref[0, rows, :] = u_gate
        gt_ref[0, 0:8, rows] = a_loc.T[MLSTM_GATE_LANE:MLSTM_GATE_LANE + 8, :]
        gt_ref[0, 8:16, rows] = u_gate.T[MLSTM_GATE_LANE:MLSTM_GATE_LANE + 8, :]
        cum_2 = cum_f * LOG2_E
        c_hi = cum_2.astype(BF16).astype(F32)
        c_mid = (cum_2 - c_hi).astype(BF16).astype(F32)
        c_lo = (cum_2 - c_hi) - c_mid
        packed = jnp.where(lane < 8, c_hi, jnp.where(lane < 16, c_mid, jnp.where(lane < 24, c_lo, 0.0))).astype(BF16)

        mvt_ref[0, :, rows] = jnp.dot(h2, wmv_ref[...], preferred_element_type=F32).T.astype(BF16)

        ext = jnp.concatenate([tail, mqk], axis=0)
        for c in range(0, 2 * MLSTM_WIDTH, MLSTM_WIDTH):
            cols = slice(c, c + MLSTM_WIDTH)
            taps = [_rows(convw_ref[t:t + 1, cols]) for t in range(CONV_WIDTH)]
            bias, out_scale = _rows(convb_ref[:, cols]), _rows(kscale_ref[:, cols])
            for r in range(0, hm, VEC_ROWS):
                y = bias + taps[3] * ext[8 + r:8 + r + VEC_ROWS, cols]
                y = y + taps[2] * ext[7 + r:7 + r + VEC_ROWS, cols]
                y = y + taps[1] * ext[6 + r:6 + r + VEC_ROWS, cols]
                y = y + taps[0] * ext[5 + r:5 + r + VEC_ROWS, cols]
                mqk_ref[0, base + r:base + r + VEC_ROWS, cols] = (jax.nn.silu(y) * out_scale).astype(BF16)

        mo_ref[0, rows, :] = jnp.dot(h2, wmo_ref[...], preferred_element_type=F32)
        q = jnp.dot(h2, wq_ref[...], preferred_element_type=F32)
        q_gate = jnp.dot(packed, pq_ref[...], preferred_element_type=F32)
        for r in range(0, hm, VEC_ROWS):
            qa_ref[0, base + r:base + r + VEC_ROWS, :] = fox_groups(q, q_gate, q_ones, slice(r, r + VEC_ROWS))
        k = jnp.dot(h2, wk_ref[...], preferred_element_type=F32)
        k_gate = jnp.dot(packed, pk_ref[...], preferred_element_type=F32)
        for r in range(0, hm, VEC_ROWS):
            ka_ref[0, base + r:base + r + VEC_ROWS, :] = fox_groups(k, k_gate, k_ones, slice(r, r + VEC_ROWS))
        return cum_f[hm - 1:hm, :], mqk[hm - 8:hm, :]

    ffn = _ffn(jnp.concatenate([norm1(0), norm1(hm)], axis=0), w1i_ref, w1o_ref)
    h2_a = residual_and_norm(0, ffn[0:hm])
    h2_b = residual_and_norm(hm, ffn[hm:])
    cum_a, tail_a = mix(0, h2_a, cum_carry[0:1, :], conv_tail[...])
    cum_b, tail_b = mix(hm, h2_b, cum_a, tail_a)
    cum_carry[0:1, :] = cum_b
    conv_tail[...] = tail_b


def _pre(x, mod, w1i, w1o, wq, wk, wv, wmqk, wmv, wmo, wg, gbias, pq, pk, cq, ck, convw, convb, kscale):
    bsz, seq, _ = x.shape
    tm = ROW_TILE
    grid = (bsz, seq // tm)
    tok = lambda width: pl.BlockSpec((1, tm, width), lambda b, j: (b, j, 0))
    feat = lambda rows: pl.BlockSpec((1, rows, tm), lambda b, j: (b, 0, j))
    consts = (w1i, w1o, wq, wk, wv, wmqk, wmv, wmo, wg, gbias, pq, pk, cq, ck, convw, convb, kscale)
    out_shape = (
        jax.ShapeDtypeStruct((bsz, seq, D_MODEL), F32),
        jax.ShapeDtypeStruct((bsz, seq, FOX_HEADS * FOX_GROUP), BF16),
        jax.ShapeDtypeStruct((bsz, seq, FOX_HEADS * FOX_GROUP), BF16),
        jax.ShapeDtypeStruct((bsz, FOX_WIDTH, seq), BF16),
        jax.ShapeDtypeStruct((bsz, seq, 2 * MLSTM_WIDTH), BF16),
        jax.ShapeDtypeStruct((bsz, MLSTM_WIDTH, seq), BF16),
        jax.ShapeDtypeStruct((bsz, seq, MLSTM_WIDTH), F32),
        jax.ShapeDtypeStruct((bsz, seq, LANES), F32),
        jax.ShapeDtypeStruct((bsz, 16, seq), F32),
    )
    out_specs = (tok(D_MODEL), tok(FOX_HEADS * FOX_GROUP), tok(FOX_HEADS * FOX_GROUP), feat(FOX_WIDTH),
                 tok(2 * MLSTM_WIDTH), feat(MLSTM_WIDTH), tok(MLSTM_WIDTH), tok(LANES), feat(16))
    return pl.pallas_call(
        functools.partial(_pre_kernel, tm=tm, lm=MLSTM_CHUNK),
        grid=grid,
        in_specs=[tok(D_MODEL), pl.BlockSpec((1, N_MOD, D_MODEL), lambda b, j: (b, 0, 0))]
        + [_resident(a.shape) for a in consts],
        out_specs=out_specs,
        out_shape=out_shape,
        scratch_shapes=[pltpu.VMEM((8, LANES), F32), pltpu.VMEM((8, 2 * MLSTM_WIDTH), F32)],
        compiler_params=pltpu.CompilerParams(dimension_semantics=("arbitrary", "arbitrary"),
                                             vmem_limit_bytes=VMEM_LIMIT),
        name="pre",
    )(x, mod, *consts)


def _fox_kernel(q_ref, k_ref, vt_ref, o_ref, s_scr, p_scr, m_scr, alpha_scr, acc_scr, *, tq):
    qi = pl.program_id(2)
    heads = range(FOX_STEP_HEADS)
    lanes = [slice(FOX_GROUP * h, FOX_GROUP * (h + 1)) for h in heads]
    rows = [slice(FOX_HEAD_DIM * h, FOX_HEAD_DIM * (h + 1)) for h in heads]

    def put_scores(h, k0):
        s_scr[h] = lax.dot_general(k_ref[0, pl.ds(k0, tq), lanes[h]], q_ref[0, :, lanes[h]],
                                   (((1,), (1,)), ((), ())), preferred_element_type=F32)

    def softmax(h, masked):
        def block(r):
            s = s_scr[h, r:r + FOX_ROW_BLOCK, :]
            if masked:
                kpos = r + lax.broadcasted_iota(jnp.int32, (FOX_ROW_BLOCK, tq), 0)
                qpos = lax.broadcasted_iota(jnp.int32, (FOX_ROW_BLOCK, tq), 1)
                s = jnp.where(kpos <= qpos, s, -jnp.inf)
            return s

        top = block(0)
        for r in range(FOX_ROW_BLOCK, tq, FOX_ROW_BLOCK):
            top = jnp.maximum(top, block(r))
        m = m_scr[h, 0:1, :]
        m_new = jnp.maximum(m, jnp.max(top, axis=0, keepdims=True))
        alpha_scr[h, 0:1, :] = jnp.exp2(m - m_new)
        m_scr[h, 0:1, :] = m_new
        for r in range(0, tq, FOX_ROW_BLOCK):
            p_scr[h, r:r + FOX_ROW_BLOCK, :] = jnp.exp2(block(r) - m_new).astype(BF16)

    def weighted_values(h, k0):
        v_aug = jnp.concatenate([vt_ref[0, rows[h], pl.ds(k0, tq)], jnp.ones((16, tq), BF16)], axis=0)
        acc_scr[h] = alpha_scr[h, 0:1, :] * acc_scr[h] + jnp.dot(v_aug, p_scr[h], preferred_element_type=F32)

    def step(t, masked):
        k0 = pl.multiple_of(t * tq, tq)
        k_prev = pl.multiple_of(jnp.maximum(k0 - tq, 0), tq)
        for g in range(0, FOX_STEP_HEADS, FOX_ISSUE_HEADS):
            group = range(g, g + FOX_ISSUE_HEADS)
            for h in group:
                put_scores(h, k0)
            for h in group:
                weighted_values(h, k_prev)
            for h in group:
                softmax(h, masked)

    m_scr[...] = jnp.full(m_scr.shape, -jnp.inf, F32)
    alpha_scr[...] = jnp.ones(alpha_scr.shape, F32)
    acc_scr[...] = jnp.zeros(acc_scr.shape, F32)
    p_scr[...] = jnp.zeros(p_scr.shape, BF16)

    @pl.loop(0, qi)
    def _(t):
        step(t, False)

    step(qi, True)
    outs = []
    for h in heads:
        weighted_values(h, pl.multiple_of(qi * tq, tq))
        a = acc_scr[h]
        outs.append(a[0:FOX_HEAD_DIM] / a[FOX_HEAD_DIM:FOX_HEAD_DIM + 1])
    o_ref[0] = jnp.concatenate(outs, axis=0).T.astype(BF16)


def _fox(qa, ka, fvt):
    bsz, seq, _ = qa.shape
    tq, nh = FOX_TILE, FOX_STEP_HEADS
    return pl.pallas_call(
        functools.partial(_fox_kernel, tq=tq),
        grid=(bsz, FOX_HEADS // nh, seq // tq),
        in_specs=[pl.BlockSpec((1, tq, nh * FOX_GROUP), lambda b, g, i: (b, i, g)),
                  pl.BlockSpec((1, seq, nh * FOX_GROUP), lambda b, g, i: (b, 0, g)),
                  pl.BlockSpec((1, nh * FOX_HEAD_DIM, seq), lambda b, g, i: (b, g, 0))],
        out_specs=pl.BlockSpec((1, tq, nh * FOX_HEAD_DIM), lambda b, g, i: (b, i, g)),
        out_shape=jax.ShapeDtypeStruct((bsz, seq, FOX_WIDTH), BF16),
        scratch_shapes=[pltpu.VMEM((nh, tq, tq), F32), pltpu.VMEM((nh, tq, tq), BF16),
                        pltpu.VMEM((nh, 8, tq), F32), pltpu.VMEM((nh, 8, tq), F32),
                        pltpu.VMEM((nh, FOX_HEAD_DIM + 16, tq), F32)],
        compiler_params=pltpu.CompilerParams(dimension_semantics=("arbitrary",) * 3,
                                             vmem_limit_bytes=VMEM_LIMIT),
        name="fox",
    )(qa, ka, fvt)


def _mlstm_kernel(q_ref, k_ref, vt_ref, mo_ref, gtok_ref, gt_ref, gain_ref, o_ref, c_scr, m_scr, *, ts, lm):
    @pl.when(pl.program_id(1) == 0)
    def _():
        c_scr[...] = jnp.zeros_like(c_scr)
        m_scr[...] = jnp.zeros_like(m_scr)

    s_idx = lax.broadcasted_iota(jnp.int32, (lm, lm), 0)
    t_idx = lax.broadcasted_iota(jnp.int32, (lm, lm), 1)
    causal = s_idx <= t_idx
    dh = MLSTM_HEAD_DIM
    for c in range(ts // lm):
        tok = slice(c * lm, (c + 1) * lm)
        for h in range(MLSTM_HEADS):
            cols = slice(dh * h, dh * (h + 1))
            q = q_ref[0, tok, cols]
            k = k_ref[0, tok, cols]
            vt = vt_ref[0, cols, tok]
            a_row = gt_ref[0, h:h + 1, tok]
            u_row = gt_ref[0, 8 + h:9 + h, tok]
            u_col = gtok_ref[0, tok, MLSTM_GATE_LANE + h:MLSTM_GATE_LANE + h + 1]
            m_prev = m_scr[h:h + 1, 0:1]
            c_prev = c_scr[h]

            g = a_row[:, lm - 1:lm]
            w_end = g + u_row
            m_new = jnp.maximum(g + m_prev, jnp.max(w_end, axis=1, keepdims=True))
            decay = jnp.exp(g + m_prev - m_new)
            wt = jnp.exp(w_end - m_new)

            d_log = jnp.where(causal, a_row + u_col, -jnp.inf)
            inter = a_row + m_prev
            m_t = jnp.maximum(inter, jnp.max(d_log, axis=0, keepdims=True))
            p = jnp.exp(d_log - m_t)
            inter_w = jnp.exp(inter - m_t)

            scores = lax.dot_general(k, q, (((1,), (1,)), ((), ())), preferred_element_type=F32) * p
            cq = lax.dot_general(c_prev.astype(BF16), q, (((1,), (1,)), ((), ())),
                                 preferred_element_type=F32)
            num = jnp.dot(vt, scores.astype(BF16), preferred_element_type=F32) + inter_w * cq[0:dh]
            den = jnp.sum(scores, axis=0, keepdims=True) + inter_w * cq[dh:dh + 1]
            out_t = num / jnp.maximum(jnp.abs(den), jnp.exp(-m_t))
            hn = out_t * lax.rsqrt(jnp.mean(out_t * out_t, axis=0, keepdims=True) + EPS)
            hb = hn.T * gain_ref[:, cols]
            o_ref[0, tok, cols] = (jax.nn.sigmoid(mo_ref[0, tok, cols]) * hb).astype(BF16)

            v_w = jnp.concatenate([vt.astype(F32) * wt, jnp.broadcast_to(wt, (16, lm))], axis=0).astype(BF16)
            c_scr[h] = decay * c_prev + jnp.dot(v_w, k, preferred_element_type=F32)
            m_scr[h:h + 1, :] = jnp.broadcast_to(m_new, (1, LANES))


def _mlstm(mqk, mvt, mo, gtok, gt, gain):
    bsz, seq, _ = mo.shape
    ts, lm = MLSTM_SEQ_TILE, MLSTM_CHUNK
    return pl.pallas_call(
        functools.partial(_mlstm_kernel, ts=ts, lm=lm),
        grid=(bsz, seq // ts),
        in_specs=[pl.BlockSpec((1, ts, MLSTM_WIDTH), lambda b, i: (b, i, 0)),
                  pl.BlockSpec((1, ts, MLSTM_WIDTH), lambda b, i: (b, i, 1)),
                  pl.BlockSpec((1, MLSTM_WIDTH, ts), lambda b, i: (b, 0, i)),
                  pl.BlockSpec((1, ts, MLSTM_WIDTH), lambda b, i: (b, i, 0)),
                  pl.BlockSpec((1, ts, LANES), lambda b, i: (b, i, 0)),
                  pl.BlockSpec((1, 16, ts), lambda b, i: (b, 0, i)),
                  pl.BlockSpec((1, MLSTM_WIDTH), lambda b, i: (0, 0))],
        out_specs=pl.BlockSpec((1, ts, MLSTM_WIDTH), lambda b, i: (b, i, 0)),
        out_shape=jax.ShapeDtypeStruct((bsz, seq, MLSTM_WIDTH), BF16),
        scratch_shapes=[pltpu.VMEM((MLSTM_HEADS, MLSTM_STATE_ROWS, MLSTM_HEAD_DIM), F32),
                        pltpu.VMEM((8, LANES), F32)],
        compiler_params=pltpu.CompilerParams(dimension_semantics=("arbitrary", "arbitrary"),
                                             vmem_limit_bytes=VMEM_LIMIT),
        name="mlstm",
    )(mqk, mqk, mvt, mo, gtok, gt, gain)


def _post_kernel(x1_ref, ya_ref, yb_ref, mod_ref, woa_ref, wob_ref, w2i_ref, w2o_ref, gf_ref, o_ref, x2_scr):
    gt2, sh3 = _rows(mod_ref[0, 5:6, :]), _rows(mod_ref[0, 6:7, :])
    sc3p, gt3h, gain = _rows(1 + mod_ref[0, 7:8, :]), _rows(0.5 * mod_ref[0, 8:9, :]), _rows(gf_ref[...])
    tm = x1_ref.shape[1]
    ya = jnp.dot(ya_ref[0], woa_ref[...], preferred_element_type=F32)
    yb = jnp.dot(yb_ref[0], wob_ref[...], preferred_element_type=F32)

    def residual_and_norm(r):
        x2 = x1_ref[0, r, :] + gt2 * (ya[r] + yb[r])
        x2_scr[r, :] = x2
        return (_rms(x2) * sc3p + sh3).astype(BF16)

    h = _by_rows(residual_and_norm, tm, VEC_ROWS)
    ffn2 = _ffn(h, w2i_ref, w2o_ref)
    for r in range(0, tm, VEC_ROWS):
        x3 = x2_scr[r:r + VEC_ROWS, :] + gt3h * ffn2[r:r + VEC_ROWS]
        o_ref[0, r:r + VEC_ROWS, :] = _rms(x3) * gain


def _post(x1, ya, yb, mod, woa, wob, w2i, w2o, gf):
    bsz, seq, _ = x1.shape
    tm = ROW_TILE
    tok = lambda width: pl.BlockSpec((1, tm, width), lambda b, j: (b, j, 0))
    consts = (woa, wob, w2i, w2o, gf)
    return pl.pallas_call(
        _post_kernel,
        grid=(bsz, seq // tm),
        in_specs=[tok(D_MODEL), tok(FOX_WIDTH), tok(MLSTM_WIDTH),
                  pl.BlockSpec((1, N_MOD, D_MODEL), lambda b, j: (b, 0, 0))]
        + [_resident(a.shape) for a in consts],
        out_specs=tok(D_MODEL),
        out_shape=jax.ShapeDtypeStruct((bsz, seq, D_MODEL), F32),
        scratch_shapes=[pltpu.VMEM((tm, D_MODEL), F32)],
        compiler_params=pltpu.CompilerParams(dimension_semantics=("arbitrary", "arbitrary"),
                                             vmem_limit_bytes=VMEM_LIMIT),
        name="post",
    )(x1, ya, yb, mod, *consts)


def _gate_placement():
    pq = np.zeros((LANES, FOX_HEADS * FOX_GROUP), np.float32)
    pk = np.zeros((LANES, FOX_HEADS * FOX_GROUP), np.float32)
    cq = np.zeros((1, FOX_HEADS * FOX_GROUP), np.float32)
    ck = np.zeros((1, FOX_HEADS * FOX_GROUP), np.float32)
    for h in range(FOX_HEADS):
        base = FOX_GROUP * h + GATE_LANE
        for part in range(3):
            pq[8 * part + h, base + part] = 1.0
            pk[8 * part + h, base + 3 + part] = -1.0
            cq[0, base + 3 + part] = 1.0
            ck[0, base + part] = 1.0
    return jnp.asarray(pq, BF16), jnp.asarray(pk, BF16), jnp.asarray(cq), jnp.asarray(ck)


def kernel(x, c, w_ada, b_ada, w_ffn1_in, w_ffn1_out, w_mix_in, b_fox_f, b_m_i, b_m_f, w_conv, b_conv, g_mhn,
           w_mix_out, w_ffn2_in, w_ffn2_out, g_final):
    assert w_ada.shape[0] == 1, "one layer"
    bsz, seq, _ = x.shape
    assert seq % max(ROW_TILE, MLSTM_SEQ_TILE, FOX_TILE) == 0

    mod = _ada(c, w_ada[0], b_ada[0]).reshape(bsz, N_MOD, D_MODEL)

    w_mix = w_mix_in[0]
    o = 0
    cols = {}
    for name, width in (("fq", FOX_WIDTH), ("fk", FOX_WIDTH), ("fv", FOX_WIDTH), ("ff", FOX_HEADS),
                        ("mq", MLSTM_WIDTH), ("mk", MLSTM_WIDTH), ("mv", MLSTM_WIDTH), ("mo", MLSTM_WIDTH),
                        ("mi", MLSTM_HEADS), ("mf", MLSTM_HEADS)):
        cols[name] = w_mix[:, o:o + width]
        o += width
    wq = (cols["fq"] * (FOX_HEAD_DIM ** -0.5 * LOG2_E)).astype(BF16)
    wk = cols["fk"].astype(BF16)
    wv = cols["fv"].astype(BF16)
    wmqk = jnp.concatenate([cols["mq"], cols["mk"]], axis=1).astype(BF16)
    wmv = cols["mv"].astype(BF16)
    wmo = cols["mo"].astype(BF16)
    zeros = lambda n: jnp.zeros((D_MODEL, n), F32)
    wg = jnp.concatenate([cols["ff"], cols["ff"], cols["ff"], cols["mi"], zeros(LANES - 28),
                          zeros(MLSTM_GATE_LANE), cols["mf"], zeros(LANES - 28)], axis=1).astype(BF16)
    zrow = lambda n: jnp.zeros((n,), F32)
    gbias = jnp.concatenate([b_fox_f[0], b_fox_f[0], b_fox_f[0], b_m_i[0], zrow(LANES - 28),
                             zrow(MLSTM_GATE_LANE), b_m_f[0], zrow(LANES - 28)]).reshape(1, 2 * LANES)
    pq, pk, cq, ck = _gate_placement()
    kscale = jnp.concatenate([jnp.ones((MLSTM_WIDTH,), F32),
                              jnp.full((MLSTM_WIDTH,), MLSTM_HEAD_DIM ** -0.5, F32)]).reshape(1, 2 * MLSTM_WIDTH)

    x1, qa, ka, fvt, mqk, mvt, mo, gtok, gt = _pre(
        x, mod, w_ffn1_in[0].astype(BF16), w_ffn1_out[0].astype(BF16), wq, wk, wv, wmqk, wmv, wmo, wg, gbias,
        pq, pk, cq, ck, w_conv[0], b_conv[0].reshape(1, -1), kscale)

    ya = _fox(qa, ka, fvt)
    yb = _mlstm(mqk, mvt, mo, gtok, gt, g_mhn[0].reshape(1, MLSTM_WIDTH))

    w_out = w_mix_out[0].astype(BF16)
    return _post(x1, ya, yb, mod, w_out[:FOX_WIDTH], w_out[FOX_WIDTH:], w_ffn2_in[0].astype(BF16),
                 w_ffn2_out[0].astype(BF16), g_final.reshape(1, D_MODEL))
```

```python
import functools

import jax
import jax.numpy as jnp
import numpy as np
from jax import lax
from jax.experimental import pallas as pl
from jax.experimental.pallas import tpu as pltpu

F32 = jnp.float32
BF16 = jnp.bfloat16

D_MODEL = 1024
D_FF = 2816
N_MOD = 9
EPS = 1e-6
LOG2_E = 1.4426950408889634
FOX_HEADS = 8
FOX_HEAD_DIM = 64
FOX_WIDTH = FOX_HEADS * FOX_HEAD_DIM
MLSTM_HEADS = 4
MLSTM_HEAD_DIM = 128
MLSTM_WIDTH = MLSTM_HEADS * MLSTM_HEAD_DIM
CONV_WIDTH = 4

LANES = 128
VEC_ROWS = 16
FOX_GROUP = 128
GATE_LANE = FOX_HEAD_DIM
MLSTM_GATE_LANE = 24
MLSTM_STATE_ROWS = MLSTM_HEAD_DIM + 16

ROW_TILE = 512
FFN_CHUNKS = (512, 512, 512, 512, 512, 256)
MLSTM_CHUNK = 256
MLSTM_SEQ_TILE = 1024
FOX_TILE = 512
FOX_STEP_HEADS = 8
FOX_ISSUE_HEADS = 2
FOX_ROW_BLOCK = 32
VMEM_LIMIT = 56 * 1024 * 1024


def _rms(x):
    return x * lax.rsqrt(jnp.mean(x * x, axis=-1, keepdims=True) + EPS)


def _by_rows(fn, rows, step):
    return jnp.concatenate([fn(slice(r, r + step)) for r in range(0, rows, step)], axis=0)


def _rows(row):
    return jnp.broadcast_to(row, (VEC_ROWS, row.shape[-1]))


def _split3(x):
    hi = x.astype(BF16)
    r1 = x - hi.astype(F32)
    mid = r1.astype(BF16)
    lo = (r1 - mid.astype(F32)).astype(BF16)
    return hi, mid, lo


def _ffn(h, w_in_ref, w_out_ref):
    acc = None
    j0 = 0
    for width in FFN_CHUNKS:
        g = jnp.dot(h, w_in_ref[:, j0:j0 + width], preferred_element_type=F32)
        u = jnp.dot(h, w_in_ref[:, D_FF + j0:D_FF + j0 + width], preferred_element_type=F32)
        a = _by_rows(lambda r: (jax.nn.silu(g[r]) * u[r]).astype(BF16), g.shape[0], 2 * VEC_ROWS)
        part = jnp.dot(a, w_out_ref[j0:j0 + width, :], preferred_element_type=F32)
        acc = part if acc is None else acc + part
        j0 += width
    return acc


def _resident(shape):
    zeros = (0,) * len(shape)
    return pl.BlockSpec(shape, lambda *_: zeros, pipeline_mode=pl.Buffered(1))


def _ada_kernel(c_ref, w_ref, b_ref, o_ref):
    c_act = jax.nn.silu(c_ref[...])
    o_ref[...] = jnp.dot(c_act, w_ref[...], preferred_element_type=F32,
                         precision=lax.Precision.HIGHEST) + b_ref[...]


def _ada(c, w, b):
    bsz = c.shape[0]
    rows = 8
    c_pad = jnp.zeros((rows, D_MODEL), F32).at[:bsz].set(c)
    n = w.shape[1]
    tn = 1536
    out = pl.pallas_call(
        _ada_kernel,
        grid=(n // tn,),
        in_specs=[pl.BlockSpec((rows, D_MODEL), lambda j: (0, 0)),
                  pl.BlockSpec((D_MODEL, tn), lambda j: (0, j)),
                  pl.BlockSpec((1, tn), lambda j: (0, j))],
        out_specs=pl.BlockSpec((rows, tn), lambda j: (0, j)),
        out_shape=jax.ShapeDtypeStruct((rows, n), F32),
        compiler_params=pltpu.CompilerParams(dimension_semantics=("arbitrary",),
                                             vmem_limit_bytes=VMEM_LIMIT),
        name="ada",
    )(c_pad, w, b.reshape(1, n))
    return out[:bsz]


def _pre_kernel(x_ref, mod_ref, w1i_ref, w1o_ref, wq_ref, wk_ref, wv_ref, wmqk_ref, wmv_ref, wmo_ref,
                wg_ref, gbias_ref, pq_ref, pk_ref, cq_ref, ck_ref, convw_ref, convb_ref, kscale_ref,
                x1_ref, qa_ref, ka_ref, fvt_ref, mqk_ref, mvt_ref, mo_ref, gtok_ref, gt_ref,
                cum_carry, conv_tail, *, tm, lm):
    hm = tm // 2
    sh1, sc1p, gt1h = _rows(mod_ref[0, 0:1, :]), _rows(1 + mod_ref[0, 1:2, :]), _rows(0.5 * mod_ref[0, 2:3, :])
    sh2, sc2p = _rows(mod_ref[0, 3:4, :]), _rows(1 + mod_ref[0, 4:5, :])

    @pl.when(pl.program_id(1) == 0)
    def _():
        cum_carry[...] = jnp.zeros_like(cum_carry)
        conv_tail[...] = jnp.zeros_like(conv_tail)

    row = lax.broadcasted_iota(jnp.int32, (hm, hm), 0)
    col = lax.broadcasted_iota(jnp.int32, (hm, hm), 1)
    dist = row - col
    tri_seq = jnp.where(dist >= 0, 1.0, 0.0).astype(BF16)
    tri_chunk = jnp.where(dist >= 0, jnp.where(dist <= (row & (lm - 1)), 1.0, 0.0), 0.0).astype(BF16)
    lane = lax.broadcasted_iota(jnp.int32, (hm, LANES), 1)
    head_lane = lax.broadcasted_iota(jnp.int32, (VEC_ROWS, FOX_GROUP), 1) < FOX_HEAD_DIM
    q_ones, k_ones = _rows(cq_ref[...]), _rows(ck_ref[...])

    def cumsum_rows(tri, v):
        hi, mid, lo = _split3(v)
        r = jnp.dot(tri, jnp.concatenate([hi, mid, lo], axis=1), preferred_element_type=F32)
        return (r[:, :LANES] + r[:, LANES:2 * LANES]) + r[:, 2 * LANES:]

    def fox_groups(proj, gate, ones, r):
        groups = []
        for pair in range(FOX_HEADS // 2):
            both = proj[r, FOX_GROUP * pair:FOX_GROUP * (pair + 1)]
            for head, vals in ((2 * pair, both), (2 * pair + 1, pltpu.roll(both, FOX_HEAD_DIM, axis=1))):
                lanes = slice(FOX_GROUP * head, FOX_GROUP * (head + 1))
                groups.append(jnp.where(head_lane, vals, gate[r, lanes] + ones[:, lanes]))
        return jnp.concatenate(groups, axis=1).astype(BF16)

    def norm1(base):
        return _by_rows(lambda r: (_rms(x_ref[0, base + r.start:base + r.stop, :]) * sc1p + sh1).astype(BF16),
                        hm, VEC_ROWS)

    def residual_and_norm(base, ffn):
        def block(r):
            rows = slice(base + r.start, base + r.stop)
            x1 = x_ref[0, rows, :] + gt1h * ffn[r]
            x1_ref[0, rows, :] = x1
            return (_rms(x1) * sc2p + sh2).astype(BF16)
        return _by_rows(block, hm, VEC_ROWS)

    def mix(base, h2, cum_in, tail):
        rows = slice(base, base + hm)
        z = jnp.dot(h2, wg_ref[...], preferred_element_type=F32) + gbias_ref[...]
        mqk = jnp.dot(h2, wmqk_ref[...], preferred_element_type=F32)
        za, zb = z[:, :LANES], z[:, LANES:]
        fvt_ref[0, :, rows] = jnp.dot(h2, wv_ref[...], preferred_element_type=F32).T.astype(BF16)

        cum_f = cumsum_rows(tri_seq, jax.nn.log_sigmoid(za)) + cum_in
        a_loc = cumsum_rows(tri_chunk, jax.nn.log_sigmoid(zb))
        u_gate = za - a_loc
        gtok_ref[0, rows, :] = u_gate
        gt_ref[0, 0:8, rows] = a_loc.T[MLSTM_GATE_LANE:MLSTM_GATE_LANE + 8, :]
        gt_ref[0, 8:16, rows] = u_gate.T[MLSTM_GATE_LANE:MLSTM_GATE_LANE + 8, :]
        cum_2 = cum_f * LOG2_E
        c_hi = cum_2.astype(BF16).astype(F32)
        c_mid = (cum_2 - c_hi).astype(BF16).astype(F32)
        c_lo = (cum_2 - c_hi) - c_mid
        packed = jnp.where(lane < 8, c_hi, jnp.where(lane < 16, c_mid, jnp.where(lane < 24, c_lo, 0.0))).astype(BF16)

        mvt_ref[0, :, rows] = jnp.dot(h2, wmv_ref[...], preferred_element_type=F32).T.astype(BF16)

        ext = jnp.concatenate([tail, mqk], axis=0)
        for c in range(0, 2 * MLSTM_WIDTH, MLSTM_WIDTH):
            cols = slice(c, c + MLSTM_WIDTH)
            taps = [_rows(convw_ref[t:t + 1, cols]) for t in range(CONV_WIDTH)]
            bias, out_scale = _rows(convb_ref[:, cols]), _rows(kscale_ref[:, cols])
            for r in range(0, hm, VEC_ROWS):
                y = bias + taps[3] * ext[8 + r:8 + r + VEC_ROWS, cols]
                y = y + taps[2] * ext[7 + r:7 + r + VEC_ROWS, cols]
                y = y + taps[1] * ext[6 + r:6 + r + VEC_ROWS, cols]
                y = y + taps[0] * ext[5 + r:5 + r + VEC_ROWS, cols]
                mqk_ref[0, base + r:base + r + VEC_ROWS, cols] = (jax.nn.silu(y) * out_scale).astype(BF16)

        mo_ref[0, rows, :] = jnp.dot(h2, wmo_ref[...], preferred_element_type=F32)
        q = jnp.dot(h2, wq_ref[...], preferred_element_type=F32)
        q_gate = jnp.dot(packed, pq_ref[...], preferred_element_type=F32)
        for r in range(0, hm, VEC_ROWS):
            qa_ref[0, base + r:base + r + VEC_ROWS, :] = fox_groups(q, q_gate, q_ones, slice(r, r + VEC_ROWS))
        k = jnp.dot(h2, wk_ref[...], preferred_element_type=F32)
        k_gate = jnp.dot(packed, pk_ref[...], preferred_element_type=F32)
        for r in range(0, hm, VEC_ROWS):
            ka_ref[0, base + r:base + r + VEC_ROWS, :] = fox_groups(k, k_gate, k_ones, slice(r, r + VEC_ROWS))
        return cum_f[hm - 1:hm, :], mqk[hm - 8:hm, :]

    ffn = _ffn(jnp.concatenate([norm1(0), norm1(hm)], axis=0), w1i_ref, w1o_ref)
    h2_a = residual_and_norm(0, ffn[0:hm])
    h2_b = residual_and_norm(hm, ffn[hm:])
    cum_a, tail_a = mix(0, h2_a, cum_carry[0:1, :], conv_tail[...])
    cum_b, tail_b = mix(hm, h2_b, cum_a, tail_a)
    cum_carry[0:1, :] = cum_b
    conv_tail[...] = tail_b


def _pre(x, mod, w1i, w1o, wq, wk, wv, wmqk, wmv, wmo, wg, gbias, pq, pk, cq, ck, convw, convb, kscale):
    bsz, seq, _ = x.shape
    tm = ROW_TILE
    grid = (bsz, seq // tm)
    tok = lambda width: pl.BlockSpec((1, tm, width), lambda b, j: (b, j, 0))
    feat = lambda rows: pl.BlockSpec((1, rows, tm), lambda b, j: (b, 0, j))
    consts = (w1i, w1o, wq, wk, wv, wmqk, wmv, wmo, wg, gbias, pq, pk, cq, ck, convw, convb, kscale)
    out_shape = (
        jax.ShapeDtypeStruct((bsz, seq, D_MODEL), F32),
        jax.ShapeDtypeStruct((bsz, seq, FOX_HEADS * FOX_GROUP), BF16),
        jax.ShapeDtypeStruct((bsz, seq, FOX_HEADS * FOX_GROUP), BF16),
        jax.ShapeDtypeStruct((bsz, FOX_WIDTH, seq), BF16),
        jax.ShapeDtypeStruct((bsz, seq, 2 * MLSTM_WIDTH), BF16),
        jax.ShapeDtypeStruct((bsz, MLSTM_WIDTH, seq), BF16),
        jax.ShapeDtypeStruct((bsz, seq, MLSTM_WIDTH), F32),
        jax.ShapeDtypeStruct((bsz, seq, LANES), F32),
        jax.ShapeDtypeStruct((bsz, 16, seq), F32),
    )
    out_specs = (tok(D_MODEL), tok(FOX_HEADS * FOX_GROUP), tok(FOX_HEADS * FOX_GROUP), feat(FOX_WIDTH),
                 tok(2 * MLSTM_WIDTH), feat(MLSTM_WIDTH), tok(MLSTM_WIDTH), tok(LANES), feat(16))
    return pl.pallas_call(
        functools.partial(_pre_kernel, tm=tm, lm=MLSTM_CHUNK),
        grid=grid,
        in_specs=[tok(D_MODEL), pl.BlockSpec((1, N_MOD, D_MODEL), lambda b, j: (b, 0, 0))]
        + [_resident(a.shape) for a in consts],
        out_specs=out_specs,
        out_shape=out_shape,
        scratch_shapes=[pltpu.VMEM((8, LANES), F32), pltpu.VMEM((8, 2 * MLSTM_WIDTH), F32)],
        compiler_params=pltpu.CompilerParams(dimension_semantics=("arbitrary", "arbitrary"),
                                             vmem_limit_bytes=VMEM_LIMIT),
        name="pre",
    )(x, mod, *consts)


def _fox_kernel(q_ref, k_ref, vt_ref, o_ref, s_scr, p_scr, m_scr, alpha_scr, acc_scr, *, tq):
    qi = pl.program_id(2)
    heads = range(FOX_STEP_HEADS)
    lanes = [slice(FOX_GROUP * h, FOX_GROUP * (h + 1)) for h in heads]
    rows = [slice(FOX_HEAD_DIM * h, FOX_HEAD_DIM * (h + 1)) for h in heads]

    def put_scores(h, k0):
        s_scr[h] = lax.dot_general(k_ref[0, pl.ds(k0, tq), lanes[h]], q_ref[0, :, lanes[h]],
                                   (((1,), (1,)), ((), ())), preferred_element_type=F32)

    def softmax(h, masked):
        def block(r):
            s = s_scr[h, r:r + FOX_ROW_BLOCK, :]
            if masked:
                kpos = r + lax.broadcasted_iota(jnp.int32, (FOX_ROW_BLOCK, tq), 0)
                qpos = lax.broadcasted_iota(jnp.int32, (FOX_ROW_BLOCK, tq), 1)
                s = jnp.where(kpos <= qpos, s, -jnp.inf)
            return s

        top = block(0)
        for r in range(FOX_ROW_BLOCK, tq, FOX_ROW_BLOCK):
            top = jnp.maximum(top, block(r))
        m = m_scr[h, 0:1, :]
        m_new = jnp.maximum(m, jnp.max(top, axis=0, keepdims=True))
        alpha_scr[h, 0:1, :] = jnp.exp2(m - m_new)
        m_scr[h, 0:1, :] = m_new
        for r in range(0, tq, FOX_ROW_BLOCK):
            p_scr[h, r:r + FOX_ROW_BLOCK, :] = jnp.exp2(block(r) - m_new).astype(BF16)

    def weighted_values(h, k0):
        v_aug = jnp.concatenate([vt_ref[0, rows[h], pl.ds(k0, tq)], jnp.ones((16, tq), BF16)], axis=0)
        acc_scr[h] = alpha_scr[h, 0:1, :] * acc_scr[h] + jnp.dot(v_aug, p_scr[h], preferred_element_type=F32)

    def step(t, masked):
        k0 = pl.multiple_of(t * tq, tq)
        k_prev = pl.multiple_of(jnp.maximum(k0 - tq, 0), tq)
        for g in range(0, FOX_STEP_HEADS, FOX_ISSUE_HEADS):
            group = range(g, g + FOX_ISSUE_HEADS)
            for h in group:
                put_scores(h, k0)
            for h in group:
                weighted_values(h, k_prev)
            for h in group:
                softmax(h, masked)

    m_scr[...] = jnp.full(m_scr.shape, -jnp.inf, F32)
    alpha_scr[...] = jnp.ones(alpha_scr.shape, F32)
    acc_scr[...] = jnp.zeros(acc_scr.shape, F32)
    p_scr[...] = jnp.zeros(p_scr.shape, BF16)

    @pl.loop(0, qi)
    def _(t):
        step(t, False)

    step(qi, True)
    outs = []
    for h in heads:
        weighted_values(h, pl.multiple_of(qi * tq, tq))
        a = acc_scr[h]
        outs.append(a[0:FOX_HEAD_DIM] / a[FOX_HEAD_DIM:FOX_HEAD_DIM + 1])
    o_ref[0] = jnp.concatenate(outs, axis=0).T.astype(BF16)


def _fox(qa, ka, fvt):
    bsz, seq, _ = qa.shape
    tq, nh = FOX_TILE, FOX_STEP_HEADS
    return pl.pallas_call(
        functools.partial(_fox_kernel, tq=tq),
        grid=(bsz, FOX_HEADS // nh, seq // tq),
        in_specs=[pl.BlockSpec((1, tq, nh * FOX_GROUP), lambda b, g, i: (b, i, g)),
                  pl.BlockSpec((1, seq, nh * FOX_GROUP), lambda b, g, i: (b, 0, g)),
                  pl.BlockSpec((1, nh * FOX_HEAD_DIM, seq), lambda b, g, i: (b, g, 0))],
        out_specs=pl.BlockSpec((1, tq, nh * FOX_HEAD_DIM), lambda b, g, i: (b, i, g)),
        out_shape=jax.ShapeDtypeStruct((bsz, seq, FOX_WIDTH), BF16),
        scratch_shapes=[pltpu.VMEM((nh, tq, tq), F32), pltpu.VMEM((nh, tq, tq), BF16),
                        pltpu.VMEM((nh, 8, tq), F32), pltpu.VMEM((nh, 8, tq), F32),
                        pltpu.VMEM((nh, FOX_HEAD_DIM + 16, tq), F32)],
        compiler_params=pltpu.CompilerParams(dimension_semantics=("arbitrary",) * 3,
                                             vmem_limit_bytes=VMEM_LIMIT),
        name="fox",
    )(qa, ka, fvt)


def _mlstm_kernel(q_ref, k_ref, vt_ref, mo_ref, gtok_ref, gt_ref, gain_ref, *rest, ts, lm):
    n_w = (len(rest) - 3) // 2
    w_f32, (o_ref, *w_bf16), (c_scr, m_scr) = rest[:n_w], rest[n_w:2 * n_w + 1], rest[2 * n_w + 1:]
    for src, dst in zip(w_f32, w_bf16):
        dst[...] = src[...].astype(BF16)

    @pl.when(pl.program_id(1) == 0)
    def _():
        c_scr[...] = jnp.zeros_like(c_scr)
        m_scr[...] = jnp.zeros_like(m_scr)

    s_idx = lax.broadcasted_iota(jnp.int32, (lm, lm), 0)
    t_idx = lax.broadcasted_iota(jnp.int32, (lm, lm), 1)
    causal = s_idx <= t_idx
    dh = MLSTM_HEAD_DIM
    for c in range(ts // lm):
        tok = slice(c * lm, (c + 1) * lm)
        for h in range(MLSTM_HEADS):
            cols = slice(dh * h, dh * (h + 1))
            q = q_ref[0, tok, cols]
            k = k_ref[0, tok, cols]
            vt = vt_ref[0, cols, tok]
            a_row = gt_ref[0, h:h + 1, tok]
            u_row = gt_ref[0, 8 + h:9 + h, tok]
            u_col = gtok_ref[0, tok, MLSTM_GATE_LANE + h:MLSTM_GATE_LANE + h + 1]
            m_prev = m_scr[h:h + 1, 0:1]
            c_prev = c_scr[h]

            g = a_row[:, lm - 1:lm]
            w_end = g + u_row
            m_new = jnp.maximum(g + m_prev, jnp.max(w_end, axis=1, keepdims=True))
            decay = jnp.exp(g + m_prev - m_new)
            wt = jnp.exp(w_end - m_new)

            d_log = jnp.where(causal, a_row + u_col, -jnp.inf)
            inter = a_row + m_prev
            m_t = jnp.maximum(inter, jnp.max(d_log, axis=0, keepdims=True))
            p = jnp.exp(d_log - m_t)
            inter_w = jnp.exp(inter - m_t)

            scores = lax.dot_general(k, q, (((1,), (1,)), ((), ())), preferred_element_type=F32) * p
            cq = lax.dot_general(c_prev.astype(BF16), q, (((1,), (1,)), ((), ())),
                                 preferred_element_type=F32)
            num = jnp.dot(vt, scores.astype(BF16), preferred_element_type=F32) + inter_w * cq[0:dh]
            den = jnp.sum(scores, axis=0, keepdims=True) + inter_w * cq[dh:dh + 1]
            out_t = num / jnp.maximum(jnp.abs(den), jnp.exp(-m_t))
            hn = out_t * lax.rsqrt(jnp.mean(out_t * out_t, axis=0, keepdims=True) + EPS)
            hb = hn.T * gain_ref[:, cols]
            o_ref[0, tok, cols] = (jax.nn.sigmoid(mo_ref[0, tok, cols]) * hb).astype(BF16)

            v_w = jnp.concatenate([vt.astype(F32) * wt, jnp.broadcast_to(wt, (16, lm))], axis=0).astype(BF16)
            c_scr[h] = decay * c_prev + jnp.dot(v_w, k, preferred_element_type=F32)
            m_scr[h:h + 1, :] = jnp.broadcast_to(m_new, (1, LANES))


def _mlstm(mqk, mvt, mo, gtok, gt, gain, post_weights):
    bsz, seq, _ = mo.shape
    ts, lm = MLSTM_SEQ_TILE, MLSTM_CHUNK
    steps = bsz * (seq // ts)
    assert all(w.shape[0] % (16 * steps) == 0 for w in post_weights), "weight rows per grid step: whole bf16 tiles"
    share = lambda w: pl.BlockSpec((w.shape[0] // steps, w.shape[1]), lambda b, i: (b * (seq // ts) + i, 0))
    out = pl.pallas_call(
        functools.partial(_mlstm_kernel, ts=ts, lm=lm),
        grid=(bsz, seq // ts),
        in_specs=[pl.BlockSpec((1, ts, MLSTM_WIDTH), lambda b, i: (b, i, 0)),
                  pl.BlockSpec((1, ts, MLSTM_WIDTH), lambda b, i: (b, i, 1)),
                  pl.BlockSpec((1, MLSTM_WIDTH, ts), lambda b, i: (b, 0, i)),
                  pl.BlockSpec((1, ts, MLSTM_WIDTH), lambda b, i: (b, i, 0)),
                  pl.BlockSpec((1, ts, LANES), lambda b, i: (b, i, 0)),
                  pl.BlockSpec((1, 16, ts), lambda b, i: (b, 0, i)),
                  pl.BlockSpec((1, MLSTM_WIDTH), lambda b, i: (0, 0))] + [share(w) for w in post_weights],
        out_specs=[pl.BlockSpec((1, ts, MLSTM_WIDTH), lambda b, i: (b, i, 0))] + [share(w) for w in post_weights],
        out_shape=[jax.ShapeDtypeStruct((bsz, seq, MLSTM_WIDTH), BF16)]
        + [jax.ShapeDtypeStruct(w.shape, BF16) for w in post_weights],
        scratch_shapes=[pltpu.VMEM((MLSTM_HEADS, MLSTM_STATE_ROWS, MLSTM_HEAD_DIM), F32),
                        pltpu.VMEM((8, LANES), F32)],
        compiler_params=pltpu.CompilerParams(dimension_semantics=("arbitrary", "arbitrary"),
                                             vmem_limit_bytes=VMEM_LIMIT),
        name="mlstm",
    )(mqk, mqk, mvt, mo, gtok, gt, gain, *post_weights)
    return out[0], out[1:]


def _post_kernel(x1_ref, ya_ref, yb_ref, mod_ref, wo_ref, w2i_ref, w2o_ref, gf_ref, o_ref, x2_scr):
    gt2, sh3 = _rows(mod_ref[0, 5:6, :]), _rows(mod_ref[0, 6:7, :])
    sc3p, gt3h, gain = _rows(1 + mod_ref[0, 7:8, :]), _rows(0.5 * mod_ref[0, 8:9, :]), _rows(gf_ref[...])
    tm = x1_ref.shape[1]
    ya = jnp.dot(ya_ref[0], wo_ref[0:FOX_WIDTH, :], preferred_element_type=F32)
    yb = jnp.dot(yb_ref[0], wo_ref[FOX_WIDTH:, :], preferred_element_type=F32)

    def residual_and_norm(r):
        x2 = x1_ref[0, r, :] + gt2 * (ya[r] + yb[r])
        x2_scr[r, :] = x2
        return (_rms(x2) * sc3p + sh3).astype(BF16)

    h = _by_rows(residual_and_norm, tm, VEC_ROWS)
    ffn2 = _ffn(h, w2i_ref, w2o_ref)
    for r in range(0, tm, VEC_ROWS):
        x3 = x2_scr[r:r + VEC_ROWS, :] + gt3h * ffn2[r:r + VEC_ROWS]
        o_ref[0, r:r + VEC_ROWS, :] = _rms(x3) * gain


def _post(x1, ya, yb, mod, wo, w2i, w2o, gf):
    bsz, seq, _ = x1.shape
    tm = ROW_TILE
    tok = lambda width: pl.BlockSpec((1, tm, width), lambda b, j: (b, j, 0))
    consts = (wo, w2i, w2o, gf)
    return pl.pallas_call(
        _post_kernel,
        grid=(bsz, seq // tm),
        in_specs=[tok(D_MODEL), tok(FOX_WIDTH), tok(MLSTM_WIDTH),
                  pl.BlockSpec((1, N_MOD, D_MODEL), lambda b, j: (b, 0, 0))]
        + [_resident(a.shape) for a in consts],
        out_specs=tok(D_MODEL),
        out_shape=jax.ShapeDtypeStruct((bsz, seq, D_MODEL), F32),
        scratch_shapes=[pltpu.VMEM((tm, D_MODEL), F32)],
        compiler_params=pltpu.CompilerParams(dimension_semantics=("arbitrary", "arbitrary"),
                                             vmem_limit_bytes=VMEM_LIMIT),
        name="post",
    )(x1, ya, yb, mod, *consts)


def _gate_placement():
    pq = np.zeros((LANES, FOX_HEADS * FOX_GROUP), np.float32)
    pk = np.zeros((LANES, FOX_HEADS * FOX_GROUP), np.float32)
    cq = np.zeros((1, FOX_HEADS * FOX_GROUP), np.float32)
    ck = np.zeros((1, FOX_HEADS * FOX_GROUP), np.float32)
    for h in range(FOX_HEADS):
        base = FOX_GROUP * h + GATE_LANE
        for part in range(3):
            pq[8 * part + h, base + part] = 1.0
            pk[8 * part + h, base + 3 + part] = -1.0
            cq[0, base + 3 + part] = 1.0
            ck[0, base + part] = 1.0
    return jnp.asarray(pq, BF16), jnp.asarray(pk, BF16), jnp.asarray(cq), jnp.asarray(ck)


def kernel(x, c, w_ada, b_ada, w_ffn1_in, w_ffn1_out, w_mix_in, b_fox_f, b_m_i, b_m_f, w_conv, b_conv, g_mhn,
           w_mix_out, w_ffn2_in, w_ffn2_out, g_final):
    assert w_ada.shape[0] == 1, "one layer"
    bsz, seq, _ = x.shape
    assert seq % max(ROW_TILE, MLSTM_SEQ_TILE, FOX_TILE) == 0

    mod = _ada(c, w_ada[0], b_ada[0]).reshape(bsz, N_MOD, D_MODEL)

    w_mix = w_mix_in[0]
    o = 0
    cols = {}
    for name, width in (("fq", FOX_WIDTH), ("fk", FOX_WIDTH), ("fv", FOX_WIDTH), ("ff", FOX_HEADS),
                        ("mq", MLSTM_WIDTH), ("mk", MLSTM_WIDTH), ("mv", MLSTM_WIDTH), ("mo", MLSTM_WIDTH),
                        ("mi", MLSTM_HEADS), ("mf", MLSTM_HEADS)):
        cols[name] = w_mix[:, o:o + width]
        o += width
    wq = (cols["fq"] * (FOX_HEAD_DIM ** -0.5 * LOG2_E)).astype(BF16)
    wk = cols["fk"].astype(BF16)
    wv = cols["fv"].astype(BF16)
    wmqk = jnp.concatenate([cols["mq"], cols["mk"]], axis=1).astype(BF16)
    wmv = cols["mv"].astype(BF16)
    wmo = cols["mo"].astype(BF16)
    zeros = lambda n: jnp.zeros((D_MODEL, n), F32)
    wg = jnp.concatenate([cols["ff"], cols["ff"], cols["ff"], cols["mi"], zeros(LANES - 28),
                          zeros(MLSTM_GATE_LANE), cols["mf"], zeros(LANES - 28)], axis=1).astype(BF16)
    zrow = lambda n: jnp.zeros((n,), F32)
    gbias = jnp.concatenate([b_fox_f[0], b_fox_f[0], b_fox_f[0], b_m_i[0], zrow(LANES - 28),
                             zrow(MLSTM_GATE_LANE), b_m_f[0], zrow(LANES - 28)]).reshape(1, 2 * LANES)
    pq, pk, cq, ck = _gate_placement()
    kscale = jnp.concatenate([jnp.ones((MLSTM_WIDTH,), F32),
                              jnp.full((MLSTM_WIDTH,), MLSTM_HEAD_DIM ** -0.5, F32)]).reshape(1, 2 * MLSTM_WIDTH)

    x1, qa, ka, fvt, mqk, mvt, mo, gtok, gt = _pre(
        x, mod, w_ffn1_in[0].astype(BF16), w_ffn1_out[0].astype(BF16), wq, wk, wv, wmqk, wmv, wmo, wg, gbias,
        pq, pk, cq, ck, w_conv[0], b_conv[0].reshape(1, -1), kscale)

    ya = _fox(qa, ka, fvt)
    yb, (w_out, w2_in, w2_out) = _mlstm(mqk, mvt, mo, gtok, gt, g_mhn[0].reshape(1, MLSTM_WIDTH),
                                        (w_mix_out[0], w_ffn2_in[0], w_ffn2_out[0]))
    return _post(x1, ya, yb, mod, w_out, w2_in, w2_out, g_final.reshape(1, D_MODEL))
```

```python
import functools

import jax
import jax.numpy as jnp
import numpy as np
from jax import lax
from jax.experimental import pallas as pl
from jax.experimental.pallas import tpu as pltpu

F32 = jnp.float32
BF16 = jnp.bfloat16

D_MODEL = 1024
D_FF = 2816
N_MOD = 9
EPS = 1e-6
LOG2_E = 1.4426950408889634
FOX_HEADS = 8
FOX_HEAD_DIM = 64
FOX_WIDTH = FOX_HEADS * FOX_HEAD_DIM
MLSTM_HEADS = 4
MLSTM_HEAD_DIM = 128
MLSTM_WIDTH = MLSTM_HEADS * MLSTM_HEAD_DIM
CONV_WIDTH = 4

LANES = 128
VEC_ROWS = 16
FOX_GROUP = 128
GATE_LANE = FOX_HEAD_DIM
MLSTM_GATE_LANE = 24
MLSTM_STATE_ROWS = MLSTM_HEAD_DIM + 16

ROW_TILE = 512
FFN_CHUNKS = (512, 512, 512, 512, 512, 256)
MLSTM_CHUNK = 256
MLSTM_SEQ_TILE = 1024
FOX_TILE = 512
FOX_STEP_HEADS = 8
FOX_ISSUE_HEADS = 2
FOX_ROW_BLOCK = 32
VMEM_LIMIT = 56 * 1024 * 1024


def _rms(x):
    return x * lax.rsqrt(jnp.mean(x * x, axis=-1, keepdims=True) + EPS)


def _by_rows(fn, rows, step):
    return jnp.concatenate([fn(slice(r, r + step)) for r in range(0, rows, step)], axis=0)


def _rows(row):
    return jnp.broadcast_to(row, (VEC_ROWS, row.shape[-1]))


def _split3(x):
    hi = x.astype(BF16)
    r1 = x - hi.astype(F32)
    mid = r1.astype(BF16)
    lo = (r1 - mid.astype(F32)).astype(BF16)
    return hi, mid, lo


def _ffn(h, w_in_ref, w_out_ref):
    acc = None
    j0 = 0
    for width in FFN_CHUNKS:
        g = jnp.dot(h, w_in_ref[:, j0:j0 + width], preferred_element_type=F32)
        u = jnp.dot(h, w_in_ref[:, D_FF + j0:D_FF + j0 + width], preferred_element_type=F32)
        a = _by_rows(lambda r: (jax.nn.silu(g[r]) * u[r]).astype(BF16), g.shape[0], 2 * VEC_ROWS)
        part = jnp.dot(a, w_out_ref[j0:j0 + width, :], preferred_element_type=F32)
        acc = part if acc is None else acc + part
        j0 += width
    return acc


def _resident(shape):
    zeros = (0,) * len(shape)
    return pl.BlockSpec(shape, lambda *_: zeros, pipeline_mode=pl.Buffered(1))


def _ada_kernel(c_ref, w_ref, b_ref, o_ref):
    c_act = jax.nn.silu(c_ref[...])
    o_ref[...] = jnp.dot(c_act, w_ref[...], preferred_element_type=F32,
                         precision=lax.Precision.HIGHEST) + b_ref[...]


def _ada(c, w, b):
    bsz = c.shape[0]
    rows = 8
    c_pad = jnp.zeros((rows, D_MODEL), F32).at[:bsz].set(c)
    n = w.shape[1]
    tn = 1536
    out = pl.pallas_call(
        _ada_kernel,
        grid=(n // tn,),
        in_specs=[pl.BlockSpec((rows, D_MODEL), lambda j: (0, 0)),
                  pl.BlockSpec((D_MODEL, tn), lambda j: (0, j)),
                  pl.BlockSpec((1, tn), lambda j: (0, j))],
        out_specs=pl.BlockSpec((rows, tn), lambda j: (0, j)),
        out_shape=jax.ShapeDtypeStruct((rows, n), F32),
        compiler_params=pltpu.CompilerParams(dimension_semantics=("arbitrary",),
                                             vmem_limit_bytes=VMEM_LIMIT),
        name="ada",
    )(c_pad, w, b.reshape(1, n))
    return out[:bsz]


MIX_SEGMENT = 512


def _split_mix_kernel(wt_ref, wfox_ref, wml_ref, gates_ref):
    q_scale = FOX_HEAD_DIM ** -0.5 * LOG2_E
    for c in range(0, 3 * FOX_WIDTH, MIX_SEGMENT):
        blk = wt_ref[c:c + MIX_SEGMENT, :].T
        wfox_ref[:, c:c + MIX_SEGMENT] = (blk * q_scale if c < FOX_WIDTH else blk).astype(BF16)
    first = 3 * FOX_WIDTH + FOX_HEADS
    for c in range(0, 4 * MLSTM_WIDTH, MIX_SEGMENT):
        wml_ref[:, c:c + MIX_SEGMENT] = wt_ref[first + c:first + c + MIX_SEGMENT, :].T.astype(BF16)
    gates_ref[0:FOX_HEADS, :] = wt_ref[3 * FOX_WIDTH:first, :]
    gates_ref[FOX_HEADS:, :] = wt_ref[first + 4 * MLSTM_WIDTH:, :]


def _split_mix(w_mix_t):
    n, d = w_mix_t.shape
    return pl.pallas_call(
        _split_mix_kernel,
        out_shape=[jax.ShapeDtypeStruct((d, 3 * FOX_WIDTH), BF16), jax.ShapeDtypeStruct((d, 4 * MLSTM_WIDTH), BF16),
                   jax.ShapeDtypeStruct((FOX_HEADS + 2 * MLSTM_HEADS, d), F32)],
        compiler_params=pltpu.CompilerParams(vmem_limit_bytes=VMEM_LIMIT),
        name="split_mix",
    )(w_mix_t)


def _pre_kernel(x_ref, mod_ref, w1i_ref, w1o_ref, wfox_ref, wml_ref,
                wg_ref, gbias_ref, pq_ref, pk_ref, cq_ref, ck_ref, convw_ref, convb_ref, kscale_ref,
                x1_ref, qa_ref, ka_ref, fvt_ref, mqk_ref, mvt_ref, mo_ref, gtok_ref, gt_ref,
                cum_carry, conv_tail, *, tm, lm):
    hm = tm // 2
    sh1, sc1p, gt1h = _rows(mod_ref[0, 0:1, :]), _rows(1 + mod_ref[0, 1:2, :]), _rows(0.5 * mod_ref[0, 2:3, :])
    sh2, sc2p = _rows(mod_ref[0, 3:4, :]), _rows(1 + mod_ref[0, 4:5, :])

    @pl.when(pl.program_id(1) == 0)
    def _():
        cum_carry[...] = jnp.zeros_like(cum_carry)
        conv_tail[...] = jnp.zeros_like(conv_tail)

    row = lax.broadcasted_iota(jnp.int32, (hm, hm), 0)
    col = lax.broadcasted_iota(jnp.int32, (hm, hm), 1)
    dist = row - col
    tri_seq = jnp.where(dist >= 0, 1.0, 0.0).astype(BF16)
    tri_chunk = jnp.where(dist >= 0, jnp.where(dist <= (row & (lm - 1)), 1.0, 0.0), 0.0).astype(BF16)
    lane = lax.broadcasted_iota(jnp.int32, (hm, LANES), 1)
    head_lane = lax.broadcasted_iota(jnp.int32, (VEC_ROWS, FOX_GROUP), 1) < FOX_HEAD_DIM
    q_ones, k_ones = _rows(cq_ref[...]), _rows(ck_ref[...])

    def cumsum_rows(tri, v):
        hi, mid, lo = _split3(v)
        r = jnp.dot(tri, jnp.concatenate([hi, mid, lo], axis=1), preferred_element_type=F32)
        return (r[:, :LANES] + r[:, LANES:2 * LANES]) + r[:, 2 * LANES:]

    def fox_groups(proj, gate, ones, r):
        groups = []
        for pair in range(FOX_HEADS // 2):
            both = proj[r, FOX_GROUP * pair:FOX_GROUP * (pair + 1)]
            for head, vals in ((2 * pair, both), (2 * pair + 1, pltpu.roll(both, FOX_HEAD_DIM, axis=1))):
                lanes = slice(FOX_GROUP * head, FOX_GROUP * (head + 1))
                groups.append(jnp.where(head_lane, vals, gate[r, lanes] + ones[:, lanes]))
        return jnp.concatenate(groups, axis=1).astype(BF16)

    def norm1(base):
        return _by_rows(lambda r: (_rms(x_ref[0, base + r.start:base + r.stop, :]) * sc1p + sh1).astype(BF16),
                        hm, VEC_ROWS)

    def residual_and_norm(base, ffn):
        def block(r):
            rows = slice(base + r.start, base + r.stop)
            x1 = x_ref[0, rows, :] + gt1h * ffn[r]
            x1_ref[0, rows, :] = x1
            return (_rms(x1) * sc2p + sh2).astype(BF16)
        return _by_rows(block, hm, VEC_ROWS)

    def mix(base, h2, cum_in, tail):
        rows = slice(base, base + hm)
        z = jnp.dot(h2, wg_ref[...], preferred_element_type=F32) + gbias_ref[...]
        mqk = jnp.dot(h2, wml_ref[:, 0:2 * MLSTM_WIDTH], preferred_element_type=F32)
        za, zb = z[:, :LANES], z[:, LANES:]
        fvt_ref[0, :, rows] = jnp.dot(h2, wfox_ref[:, 2 * FOX_WIDTH:], preferred_element_type=F32).T.astype(BF16)

        cum_f = cumsum_rows(tri_seq, jax.nn.log_sigmoid(za)) + cum_in
        a_loc = cumsum_rows(tri_chunk, jax.nn.log_sigmoid(zb))
        u_gate = za - a_loc
        gtok_ref[0, rows, :] = u_gate
        gt_ref[0, 0:8, rows] = a_loc.T[MLSTM_GATE_LANE:MLSTM_GATE_LANE + 8, :]
        gt_ref[0, 8:16, rows] = u_gate.T[MLSTM_GATE_LANE:MLSTM_GATE_LANE + 8, :]
        cum_2 = cum_f * LOG2_E
        c_hi = cum_2.astype(BF16).astype(F32)
        c_mid = (cum_2 - c_hi).astype(BF16).astype(F32)
        c_lo = (cum_2 - c_hi) - c_mid
        packed = jnp.where(lane < 8, c_hi, jnp.where(lane < 16, c_mid, jnp.where(lane < 24, c_lo, 0.0))).astype(BF16)

        mvt_ref[0, :, rows] = jnp.dot(h2, wml_ref[:, 2 * MLSTM_WIDTH:3 * MLSTM_WIDTH], preferred_element_type=F32).T.astype(BF16)

        ext = jnp.concatenate([tail, mqk], axis=0)
        for c in range(0, 2 * MLSTM_WIDTH, MLSTM_WIDTH):
            cols = slice(c, c + MLSTM_WIDTH)
            taps = [_rows(convw_ref[t:t + 1, cols]) for t in range(CONV_WIDTH)]
            bias, out_scale = _rows(convb_ref[:, cols]), _rows(kscale_ref[:, cols])
            for r in range(0, hm, VEC_ROWS):
                y = bias + taps[3] * ext[8 + r:8 + r + VEC_ROWS, cols]
                y = y + taps[2] * ext[7 + r:7 + r + VEC_ROWS, cols]
                y = y + taps[1] * ext[6 + r:6 + r + VEC_ROWS, cols]
                y = y + taps[0] * ext[5 + r:5 + r + VEC_ROWS, cols]
                mqk_ref[0, base + r:base + r + VEC_ROWS, cols] = (jax.nn.silu(y) * out_scale).astype(BF16)

        mo_ref[0, rows, :] = jnp.dot(h2, wml_ref[:, 3 * MLSTM_WIDTH:], preferred_element_type=F32)
        q = jnp.dot(h2, wfox_ref[:, 0:FOX_WIDTH], preferred_element_type=F32)
        q_gate = jnp.dot(packed, pq_ref[...], preferred_element_type=F32)
        for r in range(0, hm, VEC_ROWS):
            qa_ref[0, base + r:base + r + VEC_ROWS, :] = fox_groups(q, q_gate, q_ones, slice(r, r + VEC_ROWS))
        k = jnp.dot(h2, wfox_ref[:, FOX_WIDTH:2 * FOX_WIDTH], preferred_element_type=F32)
        k_gate = jnp.dot(packed, pk_ref[...], preferred_element_type=F32)
        for r in range(0, hm, VEC_ROWS):
            ka_ref[0, base + r:base + r + VEC_ROWS, :] = fox_groups(k, k_gate, k_ones, slice(r, r + VEC_ROWS))
        return cum_f[hm - 1:hm, :], mqk[hm - 8:hm, :]

    ffn = _ffn(jnp.concatenate([norm1(0), norm1(hm)], axis=0), w1i_ref, w1o_ref)
    h2_a = residual_and_norm(0, ffn[0:hm])
    h2_b = residual_and_norm(hm, ffn[hm:])
    cum_a, tail_a = mix(0, h2_a, cum_carry[0:1, :], conv_tail[...])
    cum_b, tail_b = mix(hm, h2_b, cum_a, tail_a)
    cum_carry[0:1, :] = cum_b
    conv_tail[...] = tail_b


def _pre(x, mod, w1i, w1o, wfox, wml, wg, gbias, pq, pk, cq, ck, convw, convb, kscale):
    bsz, seq, _ = x.shape
    tm = ROW_TILE
    grid = (bsz, seq // tm)
    tok = lambda width: pl.BlockSpec((1, tm, width), lambda b, j: (b, j, 0))
    feat = lambda rows: pl.BlockSpec((1, rows, tm), lambda b, j: (b, 0, j))
    consts = (w1i, w1o, wfox, wml, wg, gbias, pq, pk, cq, ck, convw, convb, kscale)
    out_shape = (
        jax.ShapeDtypeStruct((bsz, seq, D_MODEL), F32),
        jax.ShapeDtypeStruct((bsz, seq, FOX_HEADS * FOX_GROUP), BF16),
        jax.ShapeDtypeStruct((bsz, seq, FOX_HEADS * FOX_GROUP), BF16),
        jax.ShapeDtypeStruct((bsz, FOX_WIDTH, seq), BF16),
        jax.ShapeDtypeStruct((bsz, seq, 2 * MLSTM_WIDTH), BF16),
        jax.ShapeDtypeStruct((bsz, MLSTM_WIDTH, seq), BF16),
        jax.ShapeDtypeStruct((bsz, seq, MLSTM_WIDTH), F32),
        jax.ShapeDtypeStruct((bsz, seq, LANES), F32),
        jax.ShapeDtypeStruct((bsz, 16, seq), F32),
    )
    out_specs = (tok(D_MODEL), tok(FOX_HEADS * FOX_GROUP), tok(FOX_HEADS * FOX_GROUP), feat(FOX_WIDTH),
                 tok(2 * MLSTM_WIDTH), feat(MLSTM_WIDTH), tok(MLSTM_WIDTH), tok(LANES), feat(16))
    return pl.pallas_call(
        functools.partial(_pre_kernel, tm=tm, lm=MLSTM_CHUNK),
        grid=grid,
        in_specs=[tok(D_MODEL), pl.BlockSpec((1, N_MOD, D_MODEL), lambda b, j: (b, 0, 0))]
        + [_resident(a.shape) for a in consts],
        out_specs=out_specs,
        out_shape=out_shape,
        scratch_shapes=[pltpu.VMEM((8, LANES), F32), pltpu.VMEM((8, 2 * MLSTM_WIDTH), F32)],
        compiler_params=pltpu.CompilerParams(dimension_semantics=("arbitrary", "arbitrary"),
                                             vmem_limit_bytes=VMEM_LIMIT),
        name="pre",
    )(x, mod, *consts)


def _fox_kernel(q_ref, k_ref, vt_ref, o_ref, s_scr, p_scr, m_scr, alpha_scr, acc_scr, *, tq):
    qi = pl.program_id(2)
    heads = range(FOX_STEP_HEADS)
    lanes = [slice(FOX_GROUP * h, FOX_GROUP * (h + 1)) for h in heads]
    rows = [slice(FOX_HEAD_DIM * h, FOX_HEAD_DIM * (h + 1)) for h in heads]

    def put_scores(h, k0):
        s_scr[h] = lax.dot_general(k_ref[0, pl.ds(k0, tq), lanes[h]], q_ref[0, :, lanes[h]],
                                   (((1,), (1,)), ((), ())), preferred_element_type=F32)

    def softmax(h, masked):
        def block(r):
            s = s_scr[h, r:r + FOX_ROW_BLOCK, :]
            if masked:
                kpos = r + lax.broadcasted_iota(jnp.int32, (FOX_ROW_BLOCK, tq), 0)
                qpos = lax.broadcasted_iota(jnp.int32, (FOX_ROW_BLOCK, tq), 1)
                s = jnp.where(kpos <= qpos, s, -jnp.inf)
            return s

        top = block(0)
        for r in range(FOX_ROW_BLOCK, tq, FOX_ROW_BLOCK):
            top = jnp.maximum(top, block(r))
        m = m_scr[h, 0:1, :]
        m_new = jnp.maximum(m, jnp.max(top, axis=0, keepdims=True))
        alpha_scr[h, 0:1, :] = jnp.exp2(m - m_new)
        m_scr[h, 0:1, :] = m_new
        for r in range(0, tq, FOX_ROW_BLOCK):
            p_scr[h, r:r + FOX_ROW_BLOCK, :] = jnp.exp2(block(r) - m_new).astype(BF16)

    def weighted_values(h, k0):
        v_aug = jnp.concatenate([vt_ref[0, rows[h], pl.ds(k0, tq)], jnp.ones((16, tq), BF16)], axis=0)
        acc_scr[h] = alpha_scr[h, 0:1, :] * acc_scr[h] + jnp.dot(v_aug, p_scr[h], preferred_element_type=F32)

    def step(t, masked):
        k0 = pl.multiple_of(t * tq, tq)
        k_prev = pl.multiple_of(jnp.maximum(k0 - tq, 0), tq)
        for g in range(0, FOX_STEP_HEADS, FOX_ISSUE_HEADS):
            group = range(g, g + FOX_ISSUE_HEADS)
            for h in group:
                put_scores(h, k0)
            for h in group:
                weighted_values(h, k_prev)
            for h in group:
                softmax(h, masked)

    m_scr[...] = jnp.full(m_scr.shape, -jnp.inf, F32)
    alpha_scr[...] = jnp.ones(alpha_scr.shape, F32)
    acc_scr[...] = jnp.zeros(acc_scr.shape, F32)
    p_scr[...] = jnp.zeros(p_scr.shape, BF16)

    @pl.loop(0, qi)
    def _(t):
        step(t, False)

    step(qi, True)
    outs = []
    for h in heads:
        weighted_values(h, pl.multiple_of(qi * tq, tq))
        a = acc_scr[h]
        outs.append(a[0:FOX_HEAD_DIM] / a[FOX_HEAD_DIM:FOX_HEAD_DIM + 1])
    o_ref[0] = jnp.concatenate(outs, axis=0).T.astype(BF16)


def _fox(qa, ka, fvt):
    bsz, seq, _ = qa.shape
    tq, nh = FOX_TILE, FOX_STEP_HEADS
    return pl.pallas_call(
        functools.partial(_fox_kernel, tq=tq),
        grid=(bsz, FOX_HEADS // nh, seq // tq),
        in_specs=[pl.BlockSpec((1, tq, nh * FOX_GROUP), lambda b, g, i: (b, i, g)),
                  pl.BlockSpec((1, seq, nh * FOX_GROUP), lambda b, g, i: (b, 0, g)),
                  pl.BlockSpec((1, nh * FOX_HEAD_DIM, seq), lambda b, g, i: (b, g, 0))],
        out_specs=pl.BlockSpec((1, tq, nh * FOX_HEAD_DIM), lambda b, g, i: (b, i, g)),
        out_shape=jax.ShapeDtypeStruct((bsz, seq, FOX_WIDTH), BF16),
        scratch_shapes=[pltpu.VMEM((nh, tq, tq), F32), pltpu.VMEM((nh, tq, tq), BF16),
                        pltpu.VMEM((nh, 8, tq), F32), pltpu.VMEM((nh, 8, tq), F32),
                        pltpu.VMEM((nh, FOX_HEAD_DIM + 16, tq), F32)],
        compiler_params=pltpu.CompilerParams(dimension_semantics=("arbitrary",) * 3,
                                             vmem_limit_bytes=VMEM_LIMIT),
        name="fox",
    )(qa, ka, fvt)


def _mlstm_kernel(q_ref, k_ref, vt_ref, mo_ref, gtok_ref, gt_ref, gain_ref, *rest, ts, lm):
    n_w = (len(rest) - 3) // 2
    w_f32, (o_ref, *w_bf16), (c_scr, m_scr) = rest[:n_w], rest[n_w:2 * n_w + 1], rest[2 * n_w + 1:]
    for src, dst in zip(w_f32, w_bf16):
        dst[...] = src[...].astype(BF16)

    @pl.when(pl.program_id(1) == 0)
    def _():
        c_scr[...] = jnp.zeros_like(c_scr)
        m_scr[...] = jnp.zeros_like(m_scr)

    s_idx = lax.broadcasted_iota(jnp.int32, (lm, lm), 0)
    t_idx = lax.broadcasted_iota(jnp.int32, (lm, lm), 1)
    causal = s_idx <= t_idx
    dh = MLSTM_HEAD_DIM
    for c in range(ts // lm):
        tok = slice(c * lm, (c + 1) * lm)
        for h in range(MLSTM_HEADS):
            cols = slice(dh * h, dh * (h + 1))
            q = q_ref[0, tok, cols]
            k = k_ref[0, tok, cols]
            vt = vt_ref[0, cols, tok]
            a_row = gt_ref[0, h:h + 1, tok]
            u_row = gt_ref[0, 8 + h:9 + h, tok]
            u_col = gtok_ref[0, tok, MLSTM_GATE_LANE + h:MLSTM_GATE_LANE + h + 1]
            m_prev = m_scr[h:h + 1, 0:1]
            c_prev = c_scr[h]

            g = a_row[:, lm - 1:lm]
            w_end = g + u_row
            m_new = jnp.maximum(g + m_prev, jnp.max(w_end, axis=1, keepdims=True))
            decay = jnp.exp(g + m_prev - m_new)
            wt = jnp.exp(w_end - m_new)

            d_log = jnp.where(causal, a_row + u_col, -jnp.inf)
            inter = a_row + m_prev
            m_t = jnp.maximum(inter, jnp.max(d_log, axis=0, keepdims=True))
            p = jnp.exp(d_log - m_t)
            inter_w = jnp.exp(inter - m_t)

            scores = lax.dot_general(k, q, (((1,), (1,)), ((), ())), preferred_element_type=F32) * p
            cq = lax.dot_general(c_prev.astype(BF16), q, (((1,), (1,)), ((), ())),
                                 preferred_element_type=F32)
            num = jnp.dot(vt, scores.astype(BF16), preferred_element_type=F32) + inter_w * cq[0:dh]
            den = jnp.sum(scores, axis=0, keepdims=True) + inter_w * cq[dh:dh + 1]
            out_t = num / jnp.maximum(jnp.abs(den), jnp.exp(-m_t))
            hn = out_t * lax.rsqrt(jnp.mean(out_t * out_t, axis=0, keepdims=True) + EPS)
            hb = hn.T * gain_ref[:, cols]
            o_ref[0, tok, cols] = (jax.nn.sigmoid(mo_ref[0, tok, cols]) * hb).astype(BF16)

            v_w = jnp.concatenate([vt.astype(F32) * wt, jnp.broadcast_to(wt, (16, lm))], axis=0).astype(BF16)
            c_scr[h] = decay * c_prev + jnp.dot(v_w, k, preferred_element_type=F32)
            m_scr[h:h + 1, :] = jnp.broadcast_to(m_new, (1, LANES))


def _mlstm(mqk, mvt, mo, gtok, gt, gain, post_weights):
    bsz, seq, _ = mo.shape
    ts, lm = MLSTM_SEQ_TILE, MLSTM_CHUNK
    steps = bsz * (seq // ts)
    assert all(w.shape[0] % (16 * steps) == 0 for w in post_weights), "weight rows per grid step: whole bf16 tiles"
    share = lambda w: pl.BlockSpec((w.shape[0] // steps, w.shape[1]), lambda b, i: (b * (seq // ts) + i, 0))
    out = pl.pallas_call(
        functools.partial(_mlstm_kernel, ts=ts, lm=lm),
        grid=(bsz, seq // ts),
        in_specs=[pl.BlockSpec((1, ts, MLSTM_WIDTH), lambda b, i: (b, i, 0)),
                  pl.BlockSpec((1, ts, MLSTM_WIDTH), lambda b, i: (b, i, 1)),
                  pl.BlockSpec((1, MLSTM_WIDTH, ts), lambda b, i: (b, 0, i)),
                  pl.BlockSpec((1, ts, MLSTM_WIDTH), lambda b, i: (b, i, 0)),
                  pl.BlockSpec((1, ts, LANES), lambda b, i: (b, i, 0)),
                  pl.BlockSpec((1, 16, ts), lambda b, i: (b, 0, i)),
                  pl.BlockSpec((1, MLSTM_WIDTH), lambda b, i: (0, 0))] + [share(w) for w in post_weights],
        out_specs=[pl.BlockSpec((1, ts, MLSTM_WIDTH), lambda b, i: (b, i, 0))] + [share(w) for w in post_weights],
        out_shape=[jax.ShapeDtypeStruct((bsz, seq, MLSTM_WIDTH), BF16)]
        + [jax.ShapeDtypeStruct(w.shape, BF16) for w in post_weights],
        scratch_shapes=[pltpu.VMEM((MLSTM_HEADS, MLSTM_STATE_ROWS, MLSTM_HEAD_DIM), F32),
                        pltpu.VMEM((8, LANES), F32)],
        compiler_params=pltpu.CompilerParams(dimension_semantics=("arbitrary", "arbitrary"),
                                             vmem_limit_bytes=VMEM_LIMIT),
        name="mlstm",
    )(mqk, mqk, mvt, mo, gtok, gt, gain, *post_weights)
    return out[0], out[1:]


def _post_kernel(x1_ref, ya_ref, yb_ref, mod_ref, wo_ref, w2i_ref, w2o_ref, gf_ref, o_ref, x2_scr):
    gt2, sh3 = _rows(mod_ref[0, 5:6, :]), _rows(mod_ref[0, 6:7, :])
    sc3p, gt3h, gain = _rows(1 + mod_ref[0, 7:8, :]), _rows(0.5 * mod_ref[0, 8:9, :]), _rows(gf_ref[...])
    tm = x1_ref.shape[1]
    ya = jnp.dot(ya_ref[0], wo_ref[0:FOX_WIDTH, :], preferred_element_type=F32)
    yb = jnp.dot(yb_ref[0], wo_ref[FOX_WIDTH:, :], preferred_element_type=F32)

    def residual_and_norm(r):
        x2 = x1_ref[0, r, :] + gt2 * (ya[r] + yb[r])
        x2_scr[r, :] = x2
        return (_rms(x2) * sc3p + sh3).astype(BF16)

    h = _by_rows(residual_and_norm, tm, VEC_ROWS)
    ffn2 = _ffn(h, w2i_ref, w2o_ref)
    for r in range(0, tm, VEC_ROWS):
        x3 = x2_scr[r:r + VEC_ROWS, :] + gt3h * ffn2[r:r + VEC_ROWS]
        o_ref[0, r:r + VEC_ROWS, :] = _rms(x3) * gain


def _post(x1, ya, yb, mod, wo, w2i, w2o, gf):
    bsz, seq, _ = x1.shape
    tm = ROW_TILE
    tok = lambda width: pl.BlockSpec((1, tm, width), lambda b, j: (b, j, 0))
    consts = (wo, w2i, w2o, gf)
    return pl.pallas_call(
        _post_kernel,
        grid=(bsz, seq // tm),
        in_specs=[tok(D_MODEL), tok(FOX_WIDTH), tok(MLSTM_WIDTH),
                  pl.BlockSpec((1, N_MOD, D_MODEL), lambda b, j: (b, 0, 0))]
        + [_resident(a.shape) for a in consts],
        out_specs=tok(D_MODEL),
        out_shape=jax.ShapeDtypeStruct((bsz, seq, D_MODEL), F32),
        scratch_shapes=[pltpu.VMEM((tm, D_MODEL), F32)],
        compiler_params=pltpu.CompilerParams(dimension_semantics=("arbitrary", "arbitrary"),
                                             vmem_limit_bytes=VMEM_LIMIT),
        name="post",
    )(x1, ya, yb, mod, *consts)


def _gate_placement():
    pq = np.zeros((LANES, FOX_HEADS * FOX_GROUP), np.float32)
    pk = np.zeros((LANES, FOX_HEADS * FOX_GROUP), np.float32)
    cq = np.zeros((1, FOX_HEADS * FOX_GROUP), np.float32)
    ck = np.zeros((1, FOX_HEADS * FOX_GROUP), np.float32)
    for h in range(FOX_HEADS):
        base = FOX_GROUP * h + GATE_LANE
        for part in range(3):
            pq[8 * part + h, base + part] = 1.0
            pk[8 * part + h, base + 3 + part] = -1.0
            cq[0, base + 3 + part] = 1.0
            ck[0, base + part] = 1.0
    return jnp.asarray(pq, BF16), jnp.asarray(pk, BF16), jnp.asarray(cq), jnp.asarray(ck)


def kernel(x, c, w_ada, b_ada, w_ffn1_in, w_ffn1_out, w_mix_in, b_fox_f, b_m_i, b_m_f, w_conv, b_conv, g_mhn,
           w_mix_out, w_ffn2_in, w_ffn2_out, g_final):
    assert w_ada.shape[0] == 1, "one layer"
    bsz, seq, _ = x.shape
    assert seq % max(ROW_TILE, MLSTM_SEQ_TILE, FOX_TILE) == 0

    mod = _ada(c, w_ada[0], b_ada[0]).reshape(bsz, N_MOD, D_MODEL)

    w_mix_t = jnp.swapaxes(w_mix_in, 1, 2)[0]
    wfox, wml, gates_t = _split_mix(w_mix_t)
    ff = gates_t[0:FOX_HEADS].T
    mi = gates_t[FOX_HEADS:FOX_HEADS + MLSTM_HEADS].T
    mf = gates_t[FOX_HEADS + MLSTM_HEADS:].T
    zeros = lambda n: jnp.zeros((D_MODEL, n), F32)
    wg = jnp.concatenate([ff, ff, ff, mi, zeros(LANES - 28), zeros(MLSTM_GATE_LANE), mf, zeros(LANES - 28)],
                         axis=1).astype(BF16)
    zrow = lambda n: jnp.zeros((n,), F32)
    gbias = jnp.concatenate([b_fox_f[0], b_fox_f[0], b_fox_f[0], b_m_i[0], zrow(LANES - 28),
                             zrow(MLSTM_GATE_LANE), b_m_f[0], zrow(LANES - 28)]).reshape(1, 2 * LANES)
    pq, pk, cq, ck = _gate_placement()
    kscale = jnp.concatenate([jnp.ones((MLSTM_WIDTH,), F32),
                              jnp.full((MLSTM_WIDTH,), MLSTM_HEAD_DIM ** -0.5, F32)]).reshape(1, 2 * MLSTM_WIDTH)

    x1, qa, ka, fvt, mqk, mvt, mo, gtok, gt = _pre(
        x, mod, w_ffn1_in[0].astype(BF16), w_ffn1_out[0].astype(BF16), wfox, wml, wg, gbias,
        pq, pk, cq, ck, w_conv[0], b_conv[0].reshape(1, -1), kscale)

    ya = _fox(qa, ka, fvt)
    yb, (w_out, w2_in, w2_out) = _mlstm(mqk, mvt, mo, gtok, gt, g_mhn[0].reshape(1, MLSTM_WIDTH),
                                        (w_mix_out[0], w_ffn2_in[0], w_ffn2_out[0]))
    return _post(x1, ya, yb, mod, w_out, w2_in, w2_out, g_final.reshape(1, D_MODEL))
```

```python
import functools

import jax
import jax.numpy as jnp
import numpy as np
from jax import lax
from jax.experimental import pallas as pl
from jax.experimental.pallas import tpu as pltpu

F32 = jnp.float32
BF16 = jnp.bfloat16

D_MODEL = 1024
D_FF = 2816
N_MOD = 9
EPS = 1e-6
LOG2_E = 1.4426950408889634
FOX_HEADS = 8
FOX_HEAD_DIM = 64
FOX_WIDTH = FOX_HEADS * FOX_HEAD_DIM
MLSTM_HEADS = 4
MLSTM_HEAD_DIM = 128
MLSTM_WIDTH = MLSTM_HEADS * MLSTM_HEAD_DIM
CONV_WIDTH = 4

LANES = 128
VEC_ROWS = 16
FOX_GROUP = 128
GATE_LANE = FOX_HEAD_DIM
MLSTM_GATE_LANE = 24
MLSTM_STATE_ROWS = MLSTM_HEAD_DIM + 16

ROW_TILE = 512
FFN_CHUNKS = (512, 512, 512, 512, 512, 256)
MLSTM_CHUNK = 256
MLSTM_SEQ_TILE = 1024
FOX_TILE = 512
FOX_STEP_HEADS = 8
FOX_ISSUE_HEADS = 2
FOX_ROW_BLOCK = 32
VMEM_LIMIT = 56 * 1024 * 1024


def _rms(x):
    return x * lax.rsqrt(jnp.mean(x * x, axis=-1, keepdims=True) + EPS)


def _by_rows(fn, rows, step):
    return jnp.concatenate([fn(slice(r, r + step)) for r in range(0, rows, step)], axis=0)


def _rows(row):
    return jnp.broadcast_to(row, (VEC_ROWS, row.shape[-1]))


def _split3(x):
    hi = x.astype(BF16)
    r1 = x - hi.astype(F32)
    mid = r1.astype(BF16)
    lo = (r1 - mid.astype(F32)).astype(BF16)
    return hi, mid, lo


def _ffn(h, w_in_ref, w_out_ref):
    acc = None
    j0 = 0
    for width in FFN_CHUNKS:
        g = jnp.dot(h, w_in_ref[:, j0:j0 + width], preferred_element_type=F32)
        u = jnp.dot(h, w_in_ref[:, D_FF + j0:D_FF + j0 + width], preferred_element_type=F32)
        a = _by_rows(lambda r: (jax.nn.silu(g[r]) * u[r]).astype(BF16), g.shape[0], 2 * VEC_ROWS)
        part = jnp.dot(a, w_out_ref[j0:j0 + width, :], preferred_element_type=F32)
        acc = part if acc is None else acc + part
        j0 += width
    return acc


def _resident(shape):
    zeros = (0,) * len(shape)
    return pl.BlockSpec(shape, lambda *_: zeros, pipeline_mode=pl.Buffered(1))


def _ada_kernel(c_ref, w_ref, b_ref, o_ref):
    c_act = jax.nn.silu(c_ref[...])
    o_ref[...] = jnp.dot(c_act, w_ref[...], preferred_element_type=F32,
                         precision=lax.Precision.HIGHEST) + b_ref[...]


def _ada(c, w, b):
    bsz = c.shape[0]
    rows = 8
    c_pad = jnp.zeros((rows, D_MODEL), F32).at[:bsz].set(c)
    n = w.shape[1]
    tn = 1536
    out = pl.pallas_call(
        _ada_kernel,
        grid=(n // tn,),
        in_specs=[pl.BlockSpec((rows, D_MODEL), lambda j: (0, 0)),
                  pl.BlockSpec((D_MODEL, tn), lambda j: (0, j)),
                  pl.BlockSpec((1, tn), lambda j: (0, j))],
        out_specs=pl.BlockSpec((rows, tn), lambda j: (0, j)),
        out_shape=jax.ShapeDtypeStruct((rows, n), F32),
        compiler_params=pltpu.CompilerParams(dimension_semantics=("arbitrary",),
                                             vmem_limit_bytes=VMEM_LIMIT),
        name="ada",
    )(c_pad, w, b.reshape(1, n))
    return out[:bsz]


MIX_SEGMENT = 512


def _split_mix_kernel(wt_ref, wfox_ref, wml_ref, wg_ref):
    q_scale = FOX_HEAD_DIM ** -0.5 * LOG2_E
    for c in range(0, 3 * FOX_WIDTH, MIX_SEGMENT):
        blk = wt_ref[c:c + MIX_SEGMENT, :].T
        wfox_ref[:, c:c + MIX_SEGMENT] = (blk * q_scale if c < FOX_WIDTH else blk).astype(BF16)
    first = 3 * FOX_WIDTH + FOX_HEADS
    for c in range(0, 4 * MLSTM_WIDTH, MIX_SEGMENT):
        wml_ref[:, c:c + MIX_SEGMENT] = wt_ref[first + c:first + c + MIX_SEGMENT, :].T.astype(BF16)
    rows = jnp.concatenate([wt_ref[3 * FOX_WIDTH:first, :], wt_ref[first + 4 * MLSTM_WIDTH:, :],
                            jnp.zeros((LANES - FOX_HEADS - 2 * MLSTM_HEADS, wt_ref.shape[1]), F32)], axis=0)
    cols = rows.T
    ff, mi = cols[:, 0:FOX_HEADS], cols[:, FOX_HEADS:FOX_HEADS + MLSTM_HEADS]
    mf = cols[:, FOX_HEADS + MLSTM_HEADS:FOX_HEADS + 2 * MLSTM_HEADS]
    zeros = lambda n: jnp.zeros((cols.shape[0], n), F32)
    wg_ref[...] = jnp.concatenate([ff, ff, ff, mi, zeros(LANES - 28), zeros(MLSTM_GATE_LANE), mf, zeros(LANES - 28)],
                                  axis=1).astype(BF16)


def _split_mix(w_mix_t):
    n, d = w_mix_t.shape
    return pl.pallas_call(
        _split_mix_kernel,
        out_shape=[jax.ShapeDtypeStruct((d, 3 * FOX_WIDTH), BF16), jax.ShapeDtypeStruct((d, 4 * MLSTM_WIDTH), BF16),
                   jax.ShapeDtypeStruct((d, 2 * LANES), BF16)],
        compiler_params=pltpu.CompilerParams(vmem_limit_bytes=VMEM_LIMIT),
        name="split_mix",
    )(w_mix_t)


def _pre_kernel(x_ref, mod_ref, w1i_ref, w1o_ref, wfox_ref, wml_ref,
                wg_ref, gbias_ref, pq_ref, pk_ref, cq_ref, ck_ref, convw_ref, convb_ref, kscale_ref,
                x1_ref, qa_ref, ka_ref, fvt_ref, mqk_ref, mvt_ref, mo_ref, gtok_ref, gt_ref,
                cum_carry, conv_tail, *, tm, lm):
    hm = tm // 2
    sh1, sc1p, gt1h = _rows(mod_ref[0, 0:1, :]), _rows(1 + mod_ref[0, 1:2, :]), _rows(0.5 * mod_ref[0, 2:3, :])
    sh2, sc2p = _rows(mod_ref[0, 3:4, :]), _rows(1 + mod_ref[0, 4:5, :])

    @pl.when(pl.program_id(1) == 0)
    def _():
        cum_carry[...] = jnp.zeros_like(cum_carry)
        conv_tail[...] = jnp.zeros_like(conv_tail)

    row = lax.broadcasted_iota(jnp.int32, (hm, hm), 0)
    col = lax.broadcasted_iota(jnp.int32, (hm, hm), 1)
    dist = row - col
    tri_seq = jnp.where(dist >= 0, 1.0, 0.0).astype(BF16)
    tri_chunk = jnp.where(dist >= 0, jnp.where(dist <= (row & (lm - 1)), 1.0, 0.0), 0.0).astype(BF16)
    lane = lax.broadcasted_iota(jnp.int32, (hm, LANES), 1)
    head_lane = lax.broadcasted_iota(jnp.int32, (VEC_ROWS, FOX_GROUP), 1) < FOX_HEAD_DIM
    q_ones, k_ones = _rows(cq_ref[...]), _rows(ck_ref[...])

    def cumsum_rows(tri, v):
        hi, mid, lo = _split3(v)
        r = jnp.dot(tri, jnp.concatenate([hi, mid, lo], axis=1), preferred_element_type=F32)
        return (r[:, :LANES] + r[:, LANES:2 * LANES]) + r[:, 2 * LANES:]

    def fox_groups(proj, gate, ones, r):
        groups = []
        for pair in range(FOX_HEADS // 2):
            both = proj[r, FOX_GROUP * pair:FOX_GROUP * (pair + 1)]
            for head, vals in ((2 * pair, both), (2 * pair + 1, pltpu.roll(both, FOX_HEAD_DIM, axis=1))):
                lanes = slice(FOX_GROUP * head, FOX_GROUP * (head + 1))
                groups.append(jnp.where(head_lane, vals, gate[r, lanes] + ones[:, lanes]))
        return jnp.concatenate(groups, axis=1).astype(BF16)

    def norm1(base):
        return _by_rows(lambda r: (_rms(x_ref[0, base + r.start:base + r.stop, :]) * sc1p + sh1).astype(BF16),
                        hm, VEC_ROWS)

    def residual_and_norm(base, ffn):
        def block(r):
            rows = slice(base + r.start, base + r.stop)
            x1 = x_ref[0, rows, :] + gt1h * ffn[r]
            x1_ref[0, rows, :] = x1
            return (_rms(x1) * sc2p + sh2).astype(BF16)
        return _by_rows(block, hm, VEC_ROWS)

    def mix(base, h2, cum_in, tail):
        rows = slice(base, base + hm)
        z = jnp.dot(h2, wg_ref[...], preferred_element_type=F32) + gbias_ref[...]
        mqk = jnp.dot(h2, wml_ref[:, 0:2 * MLSTM_WIDTH], preferred_element_type=F32)
        za, zb = z[:, :LANES], z[:, LANES:]
        fvt_ref[0, :, rows] = jnp.dot(h2, wfox_ref[:, 2 * FOX_WIDTH:], preferred_element_type=F32).T.astype(BF16)

        cum_f = cumsum_rows(tri_seq, jax.nn.log_sigmoid(za)) + cum_in
        a_loc = cumsum_rows(tri_chunk, jax.nn.log_sigmoid(zb))
        u_gate = za - a_loc
        gtok_ref[0, rows, :] = u_gate
        gt_ref[0, 0:8, rows] = a_loc.T[MLSTM_GATE_LANE:MLSTM_GATE_LANE + 8, :]
        gt_ref[0, 8:16, rows] = u_gate.T[MLSTM_GATE_LANE:MLSTM_GATE_LANE + 8, :]
        cum_2 = cum_f * LOG2_E
        c_hi = cum_2.astype(BF16).astype(F32)
        c_mid = (cum_2 - c_hi).astype(BF16).astype(F32)
        c_lo = (cum_2 - c_hi) - c_mid
        packed = jnp.where(lane < 8, c_hi, jnp.where(lane < 16, c_mid, jnp.where(lane < 24, c_lo, 0.0))).astype(BF16)

        mvt_ref[0, :, rows] = jnp.dot(h2, wml_ref[:, 2 * MLSTM_WIDTH:3 * MLSTM_WIDTH], preferred_element_type=F32).T.astype(BF16)

        ext = jnp.concatenate([tail, mqk], axis=0)
        for c in range(0, 2 * MLSTM_WIDTH, MLSTM_WIDTH):
            cols = slice(c, c + MLSTM_WIDTH)
            taps = [_rows(convw_ref[t:t + 1, cols]) for t in range(CONV_WIDTH)]
            bias, out_scale = _rows(convb_ref[:, cols]), _rows(kscale_ref[:, cols])
            for r in range(0, hm, VEC_ROWS):
                y = bias + taps[3] * ext[8 + r:8 + r + VEC_ROWS, cols]
                y = y + taps[2] * ext[7 + r:7 + r + VEC_ROWS, cols]
                y = y + taps[1] * ext[6 + r:6 + r + VEC_ROWS, cols]
                y = y + taps[0] * ext[5 + r:5 + r + VEC_ROWS, cols]
                mqk_ref[0, base + r:base + r + VEC_ROWS, cols] = (jax.nn.silu(y) * out_scale).astype(BF16)

        mo_ref[0, rows, :] = jnp.dot(h2, wml_ref[:, 3 * MLSTM_WIDTH:], preferred_element_type=F32)
        q = jnp.dot(h2, wfox_ref[:, 0:FOX_WIDTH], preferred_element_type=F32)
        q_gate = jnp.dot(packed, pq_ref[...], preferred_element_type=F32)
        for r in range(0, hm, VEC_ROWS):
            qa_ref[0, base + r:base + r + VEC_ROWS, :] = fox_groups(q, q_gate, q_ones, slice(r, r + VEC_ROWS))
        k = jnp.dot(h2, wfox_ref[:, FOX_WIDTH:2 * FOX_WIDTH], preferred_element_type=F32)
        k_gate = jnp.dot(packed, pk_ref[...], preferred_element_type=F32)
        for r in range(0, hm, VEC_ROWS):
            ka_ref[0, base + r:base + r + VEC_ROWS, :] = fox_groups(k, k_gate, k_ones, slice(r, r + VEC_ROWS))
        return cum_f[hm - 1:hm, :], mqk[hm - 8:hm, :]

    ffn = _ffn(jnp.concatenate([norm1(0), norm1(hm)], axis=0), w1i_ref, w1o_ref)
    h2_a = residual_and_norm(0, ffn[0:hm])
    h2_b = residual_and_norm(hm, ffn[hm:])
    cum_a, tail_a = mix(0, h2_a, cum_carry[0:1, :], conv_tail[...])
    cum_b, tail_b = mix(hm, h2_b, cum_a, tail_a)
    cum_carry[0:1, :] = cum_b
    conv_tail[...] = tail_b


def _pre(x, mod, w1i, w1o, wfox, wml, wg, gbias, pq, pk, cq, ck, convw, convb, kscale):
    bsz, seq, _ = x.shape
    tm = ROW_TILE
    grid = (bsz, seq // tm)
    tok = lambda width: pl.BlockSpec((1, tm, width), lambda b, j: (b, j, 0))
    feat = lambda rows: pl.BlockSpec((1, rows, tm), lambda b, j: (b, 0, j))
    consts = (w1i, w1o, wfox, wml, wg, gbias, pq, pk, cq, ck, convw, convb, kscale)
    out_shape = (
        jax.ShapeDtypeStruct((bsz, seq, D_MODEL), F32),
        jax.ShapeDtypeStruct((bsz, seq, FOX_HEADS * FOX_GROUP), BF16),
        jax.ShapeDtypeStruct((bsz, seq, FOX_HEADS * FOX_GROUP), BF16),
        jax.ShapeDtypeStruct((bsz, FOX_WIDTH, seq), BF16),
        jax.ShapeDtypeStruct((bsz, seq, 2 * MLSTM_WIDTH), BF16),
        jax.ShapeDtypeStruct((bsz, MLSTM_WIDTH, seq), BF16),
        jax.ShapeDtypeStruct((bsz, seq, MLSTM_WIDTH), F32),
        jax.ShapeDtypeStruct((bsz, seq, LANES), F32),
        jax.ShapeDtypeStruct((bsz, 16, seq), F32),
    )
    out_specs = (tok(D_MODEL), tok(FOX_HEADS * FOX_GROUP), tok(FOX_HEADS * FOX_GROUP), feat(FOX_WIDTH),
                 tok(2 * MLSTM_WIDTH), feat(MLSTM_WIDTH), tok(MLSTM_WIDTH), tok(LANES), feat(16))
    return pl.pallas_call(
        functools.partial(_pre_kernel, tm=tm, lm=MLSTM_CHUNK),
        grid=grid,
        in_specs=[tok(D_MODEL), pl.BlockSpec((1, N_MOD, D_MODEL), lambda b, j: (b, 0, 0))]
        + [_resident(a.shape) for a in consts],
        out_specs=out_specs,
        out_shape=out_shape,
        scratch_shapes=[pltpu.VMEM((8, LANES), F32), pltpu.VMEM((8, 2 * MLSTM_WIDTH), F32)],
        compiler_params=pltpu.CompilerParams(dimension_semantics=("arbitrary", "arbitrary"),
                                             vmem_limit_bytes=VMEM_LIMIT),
        name="pre",
    )(x, mod, *consts)


def _fox_kernel(q_ref, k_ref, vt_ref, o_ref, s_scr, p_scr, m_scr, alpha_scr, acc_scr, *, tq):
    qi = pl.program_id(2)
    heads = range(FOX_STEP_HEADS)
    lanes = [slice(FOX_GROUP * h, FOX_GROUP * (h + 1)) for h in heads]
    rows = [slice(FOX_HEAD_DIM * h, FOX_HEAD_DIM * (h + 1)) for h in heads]

    def put_scores(h, k0):
        s_scr[h] = lax.dot_general(k_ref[0, pl.ds(k0, tq), lanes[h]], q_ref[0, :, lanes[h]],
                                   (((1,), (1,)), ((), ())), preferred_element_type=F32)

    def softmax(h, masked):
        def block(r):
            s = s_scr[h, r:r + FOX_ROW_BLOCK, :]
            if masked:
                kpos = r + lax.broadcasted_iota(jnp.int32, (FOX_ROW_BLOCK, tq), 0)
                qpos = lax.broadcasted_iota(jnp.int32, (FOX_ROW_BLOCK, tq), 1)
                s = jnp.where(kpos <= qpos, s, -jnp.inf)
            return s

        top = block(0)
        for r in range(FOX_ROW_BLOCK, tq, FOX_ROW_BLOCK):
            top = jnp.maximum(top, block(r))
        m = m_scr[h, 0:1, :]
        m_new = jnp.maximum(m, jnp.max(top, axis=0, keepdims=True))
        alpha_scr[h, 0:1, :] = jnp.exp2(m - m_new)
        m_scr[h, 0:1, :] = m_new
        for r in range(0, tq, FOX_ROW_BLOCK):
            p_scr[h, r:r + FOX_ROW_BLOCK, :] = jnp.exp2(block(r) - m_new).astype(BF16)

    def weighted_values(h, k0):
        v_aug = jnp.concatenate([vt_ref[0, rows[h], pl.ds(k0, tq)], jnp.ones((16, tq), BF16)], axis=0)
        acc_scr[h] = alpha_scr[h, 0:1, :] * acc_scr[h] + jnp.dot(v_aug, p_scr[h], preferred_element_type=F32)

    def step(t, masked):
        k0 = pl.multiple_of(t * tq, tq)
        k_prev = pl.multiple_of(jnp.maximum(k0 - tq, 0), tq)
        for g in range(0, FOX_STEP_HEADS, FOX_ISSUE_HEADS):
            group = range(g, g + FOX_ISSUE_HEADS)
            for h in group:
                put_scores(h, k0)
            for h in group:
                weighted_values(h, k_prev)
            for h in group:
                softmax(h, masked)

    m_scr[...] = jnp.full(m_scr.shape, -jnp.inf, F32)
    alpha_scr[...] = jnp.ones(alpha_scr.shape, F32)
    acc_scr[...] = jnp.zeros(acc_scr.shape, F32)
    p_scr[...] = jnp.zeros(p_scr.shape, BF16)

    @pl.loop(0, qi)
    def _(t):
        step(t, False)

    step(qi, True)
    outs = []
    for h in heads:
        weighted_values(h, pl.multiple_of(qi * tq, tq))
        a = acc_scr[h]
        outs.append(a[0:FOX_HEAD_DIM] / a[FOX_HEAD_DIM:FOX_HEAD_DIM + 1])
    o_ref[0] = jnp.concatenate(outs, axis=0).T.astype(BF16)


def _fox(qa, ka, fvt):
    bsz, seq, _ = qa.shape
    tq, nh = FOX_TILE, FOX_STEP_HEADS
    return pl.pallas_call(
        functools.partial(_fox_kernel, tq=tq),
        grid=(bsz, FOX_HEADS // nh, seq // tq),
        in_specs=[pl.BlockSpec((1, tq, nh * FOX_GROUP), lambda b, g, i: (b, i, g)),
                  pl.BlockSpec((1, seq, nh * FOX_GROUP), lambda b, g, i: (b, 0, g)),
                  pl.BlockSpec((1, nh * FOX_HEAD_DIM, seq), lambda b, g, i: (b, g, 0))],
        out_specs=pl.BlockSpec((1, tq, nh * FOX_HEAD_DIM), lambda b, g, i: (b, i, g)),
        out_shape=jax.ShapeDtypeStruct((bsz, seq, FOX_WIDTH), BF16),
        scratch_shapes=[pltpu.VMEM((nh, tq, tq), F32), pltpu.VMEM((nh, tq, tq), BF16),
                        pltpu.VMEM((nh, 8, tq), F32), pltpu.VMEM((nh, 8, tq), F32),
                        pltpu.VMEM((nh, FOX_HEAD_DIM + 16, tq), F32)],
        compiler_params=pltpu.CompilerParams(dimension_semantics=("arbitrary",) * 3,
                                             vmem_limit_bytes=VMEM_LIMIT),
        name="fox",
    )(qa, ka, fvt)


def _mlstm_kernel(q_ref, k_ref, vt_ref, mo_ref, gtok_ref, gt_ref, gain_ref, *rest, ts, lm):
    n_w = (len(rest) - 3) // 2
    w_f32, (o_ref, *w_bf16), (c_scr, m_scr) = rest[:n_w], rest[n_w:2 * n_w + 1], rest[2 * n_w + 1:]
    for src, dst in zip(w_f32, w_bf16):
        dst[...] = src[...].astype(BF16)

    @pl.when(pl.program_id(1) == 0)
    def _():
        c_scr[...] = jnp.zeros_like(c_scr)
        m_scr[...] = jnp.zeros_like(m_scr)

    s_idx = lax.broadcasted_iota(jnp.int32, (lm, lm), 0)
    t_idx = lax.broadcasted_iota(jnp.int32, (lm, lm), 1)
    causal = s_idx <= t_idx
    dh = MLSTM_HEAD_DIM
    for c in range(ts // lm):
        tok = slice(c * lm, (c + 1) * lm)
        for h in range(MLSTM_HEADS):
            cols = slice(dh * h, dh * (h + 1))
            q = q_ref[0, tok, cols]
            k = k_ref[0, tok, cols]
            vt = vt_ref[0, cols, tok]
            a_row = gt_ref[0, h:h + 1, tok]
            u_row = gt_ref[0, 8 + h:9 + h, tok]
            u_col = gtok_ref[0, tok, MLSTM_GATE_LANE + h:MLSTM_GATE_LANE + h + 1]
            m_prev = m_scr[h:h + 1, 0:1]
            c_prev = c_scr[h]

            g = a_row[:, lm - 1:lm]
            w_end = g + u_row
            m_new = jnp.maximum(g + m_prev, jnp.max(w_end, axis=1, keepdims=True))
            decay = jnp.exp(g + m_prev - m_new)
            wt = jnp.exp(w_end - m_new)

            d_log = jnp.where(causal, a_row + u_col, -jnp.inf)
            inter = a_row + m_prev
            m_t = jnp.maximum(inter, jnp.max(d_log, axis=0, keepdims=True))
            p = jnp.exp(d_log - m_t)
            inter_w = jnp.exp(inter - m_t)

            scores = lax.dot_general(k, q, (((1,), (1,)), ((), ())), preferred_element_type=F32) * p
            cq = lax.dot_general(c_prev.astype(BF16), q, (((1,), (1,)), ((), ())),
                                 preferred_element_type=F32)
            num = jnp.dot(vt, scores.astype(BF16), preferred_element_type=F32) + inter_w * cq[0:dh]
            den = jnp.sum(scores, axis=0, keepdims=True) + inter_w * cq[dh:dh + 1]
            out_t = num / jnp.maximum(jnp.abs(den), jnp.exp(-m_t))
            hn = out_t * lax.rsqrt(jnp.mean(out_t * out_t, axis=0, keepdims=True) + EPS)
            hb = hn.T * gain_ref[:, cols]
            o_ref[0, tok, cols] = (jax.nn.sigmoid(mo_ref[0, tok, cols]) * hb).astype(BF16)

            v_w = jnp.concatenate([vt.astype(F32) * wt, jnp.broadcast_to(wt, (16, lm))], axis=0).astype(BF16)
            c_scr[h] = decay * c_prev + jnp.dot(v_w, k, preferred_element_type=F32)
            m_scr[h:h + 1, :] = jnp.broadcast_to(m_new, (1, LANES))


def _mlstm(mqk, mvt, mo, gtok, gt, gain, post_weights):
    bsz, seq, _ = mo.shape
    ts, lm = MLSTM_SEQ_TILE, MLSTM_CHUNK
    steps = bsz * (seq // ts)
    assert all(w.shape[0] % (16 * steps) == 0 for w in post_weights), "weight rows per grid step: whole bf16 tiles"
    share = lambda w: pl.BlockSpec((w.shape[0] // steps, w.shape[1]), lambda b, i: (b * (seq // ts) + i, 0))
    out = pl.pallas_call(
        functools.partial(_mlstm_kernel, ts=ts, lm=lm),
        grid=(bsz, seq // ts),
        in_specs=[pl.BlockSpec((1, ts, MLSTM_WIDTH), lambda b, i: (b, i, 0)),
                  pl.BlockSpec((1, ts, MLSTM_WIDTH), lambda b, i: (b, i, 1)),
                  pl.BlockSpec((1, MLSTM_WIDTH, ts), lambda b, i: (b, 0, i)),
                  pl.BlockSpec((1, ts, MLSTM_WIDTH), lambda b, i: (b, i, 0)),
                  pl.BlockSpec((1, ts, LANES), lambda b, i: (b, i, 0)),
                  pl.BlockSpec((1, 16, ts), lambda b, i: (b, 0, i)),
                  pl.BlockSpec((1, MLSTM_WIDTH), lambda b, i: (0, 0))] + [share(w) for w in post_weights],
        out_specs=[pl.BlockSpec((1, ts, MLSTM_WIDTH), lambda b, i: (b, i, 0))] + [share(w) for w in post_weights],
        out_shape=[jax.ShapeDtypeStruct((bsz, seq, MLSTM_WIDTH), BF16)]
        + [jax.ShapeDtypeStruct(w.shape, BF16) for w in post_weights],
        scratch_shapes=[pltpu.VMEM((MLSTM_HEADS, MLSTM_STATE_ROWS, MLSTM_HEAD_DIM), F32),
                        pltpu.VMEM((8, LANES), F32)],
        compiler_params=pltpu.CompilerParams(dimension_semantics=("arbitrary", "arbitrary"),
                                             vmem_limit_bytes=VMEM_LIMIT),
        name="mlstm",
    )(mqk, mqk, mvt, mo, gtok, gt, gain, *post_weights)
    return out[0], out[1:]


def _post_kernel(x1_ref, ya_ref, yb_ref, mod_ref, wo_ref, w2i_ref, w2o_ref, gf_ref, o_ref, x2_scr):
    gt2, sh3 = _rows(mod_ref[0, 5:6, :]), _rows(mod_ref[0, 6:7, :])
    sc3p, gt3h, gain = _rows(1 + mod_ref[0, 7:8, :]), _rows(0.5 * mod_ref[0, 8:9, :]), _rows(gf_ref[...])
    tm = x1_ref.shape[1]
    ya = jnp.dot(ya_ref[0], wo_ref[0:FOX_WIDTH, :], preferred_element_type=F32)
    yb = jnp.dot(yb_ref[0], wo_ref[FOX_WIDTH:, :], preferred_element_type=F32)

    def residual_and_norm(r):
        x2 = x1_ref[0, r, :] + gt2 * (ya[r] + yb[r])
        x2_scr[r, :] = x2
        return (_rms(x2) * sc3p + sh3).astype(BF16)

    h = _by_rows(residual_and_norm, tm, VEC_ROWS)
    ffn2 = _ffn(h, w2i_ref, w2o_ref)
    for r in range(0, tm, VEC_ROWS):
        x3 = x2_scr[r:r + VEC_ROWS, :] + gt3h * ffn2[r:r + VEC_ROWS]
        o_ref[0, r:r + VEC_ROWS, :] = _rms(x3) * gain


def _post(x1, ya, yb, mod, wo, w2i, w2o, gf):
    bsz, seq, _ = x1.shape
    tm = ROW_TILE
    tok = lambda width: pl.BlockSpec((1, tm, width), lambda b, j: (b, j, 0))
    consts = (wo, w2i, w2o, gf)
    return pl.pallas_call(
        _post_kernel,
        grid=(bsz, seq // tm),
        in_specs=[tok(D_MODEL), tok(FOX_WIDTH), tok(MLSTM_WIDTH),
                  pl.BlockSpec((1, N_MOD, D_MODEL), lambda b, j: (b, 0, 0))]
        + [_resident(a.shape) for a in consts],
        out_specs=tok(D_MODEL),
        out_shape=jax.ShapeDtypeStruct((bsz, seq, D_MODEL), F32),
        scratch_shapes=[pltpu.VMEM((tm, D_MODEL), F32)],
        compiler_params=pltpu.CompilerParams(dimension_semantics=("arbitrary", "arbitrary"),
                                             vmem_limit_bytes=VMEM_LIMIT),
        name="post",
    )(x1, ya, yb, mod, *consts)


def _gate_placement():
    pq = np.zeros((LANES, FOX_HEADS * FOX_GROUP), np.float32)
    pk = np.zeros((LANES, FOX_HEADS * FOX_GROUP), np.float32)
    cq = np.zeros((1, FOX_HEADS * FOX_GROUP), np.float32)
    ck = np.zeros((1, FOX_HEADS * FOX_GROUP), np.float32)
    for h in range(FOX_HEADS):
        base = FOX_GROUP * h + GATE_LANE
        for part in range(3):
            pq[8 * part + h, base + part] = 1.0
            pk[8 * part + h, base + 3 + part] = -1.0
            cq[0, base + 3 + part] = 1.0
            ck[0, base + part] = 1.0
    return jnp.asarray(pq, BF16), jnp.asarray(pk, BF16), jnp.asarray(cq), jnp.asarray(ck)


def kernel(x, c, w_ada, b_ada, w_ffn1_in, w_ffn1_out, w_mix_in, b_fox_f, b_m_i, b_m_f, w_conv, b_conv, g_mhn,
           w_mix_out, w_ffn2_in, w_ffn2_out, g_final):
    assert w_ada.shape[0] == 1, "one layer"
    bsz, seq, _ = x.shape
    assert seq % max(ROW_TILE, MLSTM_SEQ_TILE, FOX_TILE) == 0

    mod = _ada(c, w_ada[0], b_ada[0]).reshape(bsz, N_MOD, D_MODEL)

    wfox, wml, wg = _split_mix(jnp.swapaxes(w_mix_in, 1, 2)[0])
    zrow = lambda n: jnp.zeros((n,), F32)
    gbias = jnp.concatenate([b_fox_f[0], b_fox_f[0], b_fox_f[0], b_m_i[0], zrow(LANES - 28),
                             zrow(MLSTM_GATE_LANE), b_m_f[0], zrow(LANES - 28)]).reshape(1, 2 * LANES)
    pq, pk, cq, ck = _gate_placement()
    kscale = jnp.concatenate([jnp.ones((MLSTM_WIDTH,), F32),
                              jnp.full((MLSTM_WIDTH,), MLSTM_HEAD_DIM ** -0.5, F32)]).reshape(1, 2 * MLSTM_WIDTH)

    x1, qa, ka, fvt, mqk, mvt, mo, gtok, gt = _pre(
        x, mod, w_ffn1_in[0].astype(BF16), w_ffn1_out[0].astype(BF16), wfox, wml, wg, gbias,
        pq, pk, cq, ck, w_conv[0], b_conv[0].reshape(1, -1), kscale)

    ya = _fox(qa, ka, fvt)
    yb, (w_out, w2_in, w2_out) = _mlstm(mqk, mvt, mo, gtok, gt, g_mhn[0].reshape(1, MLSTM_WIDTH),
                                        (w_mix_out[0], w_ffn2_in[0], w_ffn2_out[0]))
    return _post(x1, ya, yb, mod, w_out, w2_in, w2_out, g_final.reshape(1, D_MODEL))
```

```python
import functools

import jax
import jax.numpy as jnp
import numpy as np
from jax import lax
from jax.experimental import pallas as pl
from jax.experimental.pallas import tpu as pltpu

F32 = jnp.float32
BF16 = jnp.bfloat16

D_MODEL = 1024
D_FF = 2816
N_MOD = 9
EPS = 1e-6
LOG2_E = 1.4426950408889634
FOX_HEADS = 8
FOX_HEAD_DIM = 64
FOX_WIDTH = FOX_HEADS * FOX_HEAD_DIM
MLSTM_HEADS = 4
MLSTM_HEAD_DIM = 128
MLSTM_WIDTH = MLSTM_HEADS * MLSTM_HEAD_DIM
CONV_WIDTH = 4

LANES = 128
VEC_ROWS = 16
FOX_GROUP = 128
GATE_LANE = FOX_HEAD_DIM
MLSTM_GATE_LANE = 24
MLSTM_STATE_ROWS = MLSTM_HEAD_DIM + 16

ADA_STEPS = 8
ROW_TILE = 512
FFN_CHUNKS = (512, 512, 512, 512, 512, 256)
MLSTM_CHUNK = 256
MLSTM_SEQ_TILE = 1024
FOX_TILE = 512
FOX_STEP_HEADS = 8
FOX_ISSUE_HEADS = 2
FOX_ROW_BLOCK = 32
VMEM_LIMIT = 56 * 1024 * 1024


def _rms(x):
    return x * lax.rsqrt(jnp.mean(x * x, axis=-1, keepdims=True) + EPS)


def _by_rows(fn, rows, step):
    return jnp.concatenate([fn(slice(r, r + step)) for r in range(0, rows, step)], axis=0)


def _rows(row):
    return jnp.broadcast_to(row, (VEC_ROWS, row.shape[-1]))


def _split3(x):
    hi = x.astype(BF16)
    r1 = x - hi.astype(F32)
    mid = r1.astype(BF16)
    lo = (r1 - mid.astype(F32)).astype(BF16)
    return hi, mid, lo


def _ffn(h, w_in_ref, w_out_ref):
    acc = None
    j0 = 0
    for width in FFN_CHUNKS:
        g = jnp.dot(h, w_in_ref[:, j0:j0 + width], preferred_element_type=F32)
        u = jnp.dot(h, w_in_ref[:, D_FF + j0:D_FF + j0 + width], preferred_element_type=F32)
        a = _by_rows(lambda r: (jax.nn.silu(g[r]) * u[r]).astype(BF16), g.shape[0], 2 * VEC_ROWS)
        part = jnp.dot(a, w_out_ref[j0:j0 + width, :], preferred_element_type=F32)
        acc = part if acc is None else acc + part
        j0 += width
    return acc


def _resident(shape):
    zeros = (0,) * len(shape)
    return pl.BlockSpec(shape, lambda *_: zeros, pipeline_mode=pl.Buffered(1))


def _ada_kernel(c_ref, w_ref, b_ref, *rest):
    n_w = (len(rest) - 1) // 2
    w_f32, (o_ref, *w_bf16) = rest[:n_w], rest[n_w:]
    for src, dst in zip(w_f32, w_bf16):
        dst[...] = src[...].astype(BF16)
    c_act = jax.nn.silu(c_ref[...])
    o_ref[...] = jnp.dot(c_act, w_ref[...], preferred_element_type=F32,
                         precision=lax.Precision.HIGHEST) + b_ref[...]


def _ada(c, w, b, pre_weights):
    bsz = c.shape[0]
    rows = 8
    c_pad = jnp.zeros((rows, D_MODEL), F32).at[:bsz].set(c)
    n = w.shape[1]
    steps = ADA_STEPS
    tn = n // steps
    assert all(p.shape[0] % (16 * steps) == 0 for p in pre_weights), "weight rows per grid step: whole bf16 tiles"
    share = lambda p: pl.BlockSpec((p.shape[0] // steps, p.shape[1]), lambda j: (j, 0))
    out = pl.pallas_call(
        _ada_kernel,
        grid=(steps,),
        in_specs=[pl.BlockSpec((rows, D_MODEL), lambda j: (0, 0)),
                  pl.BlockSpec((D_MODEL, tn), lambda j: (0, j)),
                  pl.BlockSpec((1, tn), lambda j: (0, j))] + [share(p) for p in pre_weights],
        out_specs=[pl.BlockSpec((rows, tn), lambda j: (0, j))] + [share(p) for p in pre_weights],
        out_shape=[jax.ShapeDtypeStruct((rows, n), F32)] + [jax.ShapeDtypeStruct(p.shape, BF16) for p in pre_weights],
        compiler_params=pltpu.CompilerParams(dimension_semantics=("arbitrary",),
                                             vmem_limit_bytes=VMEM_LIMIT),
        name="ada",
    )(c_pad, w, b.reshape(1, n), *pre_weights)
    return out[0][:bsz], out[1:]


MIX_SEGMENT = 512


def _split_mix_kernel(wt_ref, wfox_ref, wml_ref, wg_ref):
    q_scale = FOX_HEAD_DIM ** -0.5 * LOG2_E
    for c in range(0, 3 * FOX_WIDTH, MIX_SEGMENT):
        blk = wt_ref[c:c + MIX_SEGMENT, :].T
        wfox_ref[:, c:c + MIX_SEGMENT] = (blk * q_scale if c < FOX_WIDTH else blk).astype(BF16)
    first = 3 * FOX_WIDTH + FOX_HEADS
    for c in range(0, 4 * MLSTM_WIDTH, MIX_SEGMENT):
        wml_ref[:, c:c + MIX_SEGMENT] = wt_ref[first + c:first + c + MIX_SEGMENT, :].T.astype(BF16)
    rows = jnp.concatenate([wt_ref[3 * FOX_WIDTH:first, :], wt_ref[first + 4 * MLSTM_WIDTH:, :],
                            jnp.zeros((LANES - FOX_HEADS - 2 * MLSTM_HEADS, wt_ref.shape[1]), F32)], axis=0)
    cols = rows.T
    ff, mi = cols[:, 0:FOX_HEADS], cols[:, FOX_HEADS:FOX_HEADS + MLSTM_HEADS]
    mf = cols[:, FOX_HEADS + MLSTM_HEADS:FOX_HEADS + 2 * MLSTM_HEADS]
    zeros = lambda n: jnp.zeros((cols.shape[0], n), F32)
    wg_ref[...] = jnp.concatenate([ff, ff, ff, mi, zeros(LANES - 28), zeros(MLSTM_GATE_LANE), mf, zeros(LANES - 28)],
                                  axis=1).astype(BF16)


def _split_mix(w_mix_t):
    n, d = w_mix_t.shape
    return pl.pallas_call(
        _split_mix_kernel,
        out_shape=[jax.ShapeDtypeStruct((d, 3 * FOX_WIDTH), BF16), jax.ShapeDtypeStruct((d, 4 * MLSTM_WIDTH), BF16),
                   jax.ShapeDtypeStruct((d, 2 * LANES), BF16)],
        compiler_params=pltpu.CompilerParams(vmem_limit_bytes=VMEM_LIMIT),
        name="split_mix",
    )(w_mix_t)


def _pre_kernel(x_ref, mod_ref, w1i_ref, w1o_ref, wfox_ref, wml_ref,
                wg_ref, gbias_ref, pq_ref, pk_ref, cq_ref, ck_ref, convw_ref, convb_ref, kscale_ref,
                x1_ref, qa_ref, ka_ref, fvt_ref, mqk_ref, mvt_ref, mo_ref, gtok_ref, gt_ref,
                cum_carry, conv_tail, *, tm, lm):
    hm = tm // 2
    sh1, sc1p, gt1h = _rows(mod_ref[0, 0:1, :]), _rows(1 + mod_ref[0, 1:2, :]), _rows(0.5 * mod_ref[0, 2:3, :])
    sh2, sc2p = _rows(mod_ref[0, 3:4, :]), _rows(1 + mod_ref[0, 4:5, :])

    @pl.when(pl.program_id(1) == 0)
    def _():
        cum_carry[...] = jnp.zeros_like(cum_carry)
        conv_tail[...] = jnp.zeros_like(conv_tail)

    row = lax.broadcasted_iota(jnp.int32, (hm, hm), 0)
    col = lax.broadcasted_iota(jnp.int32, (hm, hm), 1)
    dist = row - col
    tri_seq = jnp.where(dist >= 0, 1.0, 0.0).astype(BF16)
    tri_chunk = jnp.where(dist >= 0, jnp.where(dist <= (row & (lm - 1)), 1.0, 0.0), 0.0).astype(BF16)
    lane = lax.broadcasted_iota(jnp.int32, (hm, LANES), 1)
    head_lane = lax.broadcasted_iota(jnp.int32, (VEC_ROWS, FOX_GROUP), 1) < FOX_HEAD_DIM
    q_ones, k_ones = _rows(cq_ref[...]), _rows(ck_ref[...])

    def cumsum_rows(tri, v):
        hi, mid, lo = _split3(v)
        r = jnp.dot(tri, jnp.concatenate([hi, mid, lo], axis=1), preferred_element_type=F32)
        return (r[:, :LANES] + r[:, LANES:2 * LANES]) + r[:, 2 * LANES:]

    def fox_groups(proj, gate, ones, r):
        groups = []
        for pair in range(FOX_HEADS // 2):
            both = proj[r, FOX_GROUP * pair:FOX_GROUP * (pair + 1)]
            for head, vals in ((2 * pair, both), (2 * pair + 1, pltpu.roll(both, FOX_HEAD_DIM, axis=1))):
                lanes = slice(FOX_GROUP * head, FOX_GROUP * (head + 1))
                groups.append(jnp.where(head_lane, vals, gate[r, lanes] + ones[:, lanes]))
        return jnp.concatenate(groups, axis=1).astype(BF16)

    def norm1(base):
        return _by_rows(lambda r: (_rms(x_ref[0, base + r.start:base + r.stop, :]) * sc1p + sh1).astype(BF16),
                        hm, VEC_ROWS)

    def residual_and_norm(base, ffn):
        def block(r):
            rows = slice(base + r.start, base + r.stop)
            x1 = x_ref[0, rows, :] + gt1h * ffn[r]
            x1_ref[0, rows, :] = x1
            return (_rms(x1) * sc2p + sh2).astype(BF16)
        return _by_rows(block, hm, VEC_ROWS)

    def mix(base, h2, cum_in, tail):
        rows = slice(base, base + hm)
        z = jnp.dot(h2, wg_ref[...], preferred_element_type=F32) + gbias_ref[...]
        mqk = jnp.dot(h2, wml_ref[:, 0:2 * MLSTM_WIDTH], preferred_element_type=F32)
        za, zb = z[:, :LANES], z[:, LANES:]
        fvt_ref[0, :, rows] = jnp.dot(h2, wfox_ref[:, 2 * FOX_WIDTH:], preferred_element_type=F32).T.astype(BF16)

        cum_f = cumsum_rows(tri_seq, jax.nn.log_sigmoid(za)) + cum_in
        a_loc = cumsum_rows(tri_chunk, jax.nn.log_sigmoid(zb))
        u_gate = za - a_loc
        gtok_ref[0, rows, :] = u_gate
        gt_ref[0, 0:8, rows] = a_loc.T[MLSTM_GATE_LANE:MLSTM_GATE_LANE + 8, :]
        gt_ref[0, 8:16, rows] = u_gate.T[MLSTM_GATE_LANE:MLSTM_GATE_LANE + 8, :]
        cum_2 = cum_f * LOG2_E
        c_hi = cum_2.astype(BF16).astype(F32)
        c_mid = (cum_2 - c_hi).astype(BF16).astype(F32)
        c_lo = (cum_2 - c_hi) - c_mid
        packed = jnp.where(lane < 8, c_hi, jnp.where(lane < 16, c_mid, jnp.where(lane < 24, c_lo, 0.0))).astype(BF16)

        mvt_ref[0, :, rows] = jnp.dot(h2, wml_ref[:, 2 * MLSTM_WIDTH:3 * MLSTM_WIDTH], preferred_element_type=F32).T.astype(BF16)

        ext = jnp.concatenate([tail, mqk], axis=0)
        for c in range(0, 2 * MLSTM_WIDTH, MLSTM_WIDTH):
            cols = slice(c, c + MLSTM_WIDTH)
            taps = [_rows(convw_ref[t:t + 1, cols]) for t in range(CONV_WIDTH)]
            bias, out_scale = _rows(convb_ref[:, cols]), _rows(kscale_ref[:, cols])
            for r in range(0, hm, VEC_ROWS):
                y = bias + taps[3] * ext[8 + r:8 + r + VEC_ROWS, cols]
                y = y + taps[2] * ext[7 + r:7 + r + VEC_ROWS, cols]
                y = y + taps[1] * ext[6 + r:6 + r + VEC_ROWS, cols]
                y = y + taps[0] * ext[5 + r:5 + r + VEC_ROWS, cols]
                mqk_ref[0, base + r:base + r + VEC_ROWS, cols] = (jax.nn.silu(y) * out_scale).astype(BF16)

        mo_ref[0, rows, :] = jnp.dot(h2, wml_ref[:, 3 * MLSTM_WIDTH:], preferred_element_type=F32)
        q = jnp.dot(h2, wfox_ref[:, 0:FOX_WIDTH], preferred_element_type=F32)
        q_gate = jnp.dot(packed, pq_ref[...], preferred_element_type=F32)
        for r in range(0, hm, VEC_ROWS):
            qa_ref[0, base + r:base + r + VEC_ROWS, :] = fox_groups(q, q_gate, q_ones, slice(r, r + VEC_ROWS))
        k = jnp.dot(h2, wfox_ref[:, FOX_WIDTH:2 * FOX_WIDTH], preferred_element_type=F32)
        k_gate = jnp.dot(packed, pk_ref[...], preferred_element_type=F32)
        for r in range(0, hm, VEC_ROWS):
            ka_ref[0, base + r:base + r + VEC_ROWS, :] = fox_groups(k, k_gate, k_ones, slice(r, r + VEC_ROWS))
        return cum_f[hm - 1:hm, :], mqk[hm - 8:hm, :]

    ffn = _ffn(jnp.concatenate([norm1(0), norm1(hm)], axis=0), w1i_ref, w1o_ref)
    h2_a = residual_and_norm(0, ffn[0:hm])
    h2_b = residual_and_norm(hm, ffn[hm:])
    cum_a, tail_a = mix(0, h2_a, cum_carry[0:1, :], conv_tail[...])
    cum_b, tail_b = mix(hm, h2_b, cum_a, tail_a)
    cum_carry[0:1, :] = cum_b
    conv_tail[...] = tail_b


def _pre(x, mod, w1i, w1o, wfox, wml, wg, gbias, pq, pk, cq, ck, convw, convb, kscale):
    bsz, seq, _ = x.shape
    tm = ROW_TILE
    grid = (bsz, seq // tm)
    tok = lambda width: pl.BlockSpec((1, tm, width), lambda b, j: (b, j, 0))
    feat = lambda rows: pl.BlockSpec((1, rows, tm), lambda b, j: (b, 0, j))
    consts = (w1i, w1o, wfox, wml, wg, gbias, pq, pk, cq, ck, convw, convb, kscale)
    out_shape = (
        jax.ShapeDtypeStruct((bsz, seq, D_MODEL), F32),
        jax.ShapeDtypeStruct((bsz, seq, FOX_HEADS * FOX_GROUP), BF16),
        jax.ShapeDtypeStruct((bsz, seq, FOX_HEADS * FOX_GROUP), BF16),
        jax.ShapeDtypeStruct((bsz, FOX_WIDTH, seq), BF16),
        jax.ShapeDtypeStruct((bsz, seq, 2 * MLSTM_WIDTH), BF16),
        jax.ShapeDtypeStruct((bsz, MLSTM_WIDTH, seq), BF16),
        jax.ShapeDtypeStruct((bsz, seq, MLSTM_WIDTH), F32),
        jax.ShapeDtypeStruct((bsz, seq, LANES), F32),
        jax.ShapeDtypeStruct((bsz, 16, seq), F32),
    )
    out_specs = (tok(D_MODEL), tok(FOX_HEADS * FOX_GROUP), tok(FOX_HEADS * FOX_GROUP), feat(FOX_WIDTH),
                 tok(2 * MLSTM_WIDTH), feat(MLSTM_WIDTH), tok(MLSTM_WIDTH), tok(LANES), feat(16))
    return pl.pallas_call(
        functools.partial(_pre_kernel, tm=tm, lm=MLSTM_CHUNK),
        grid=grid,
        in_specs=[tok(D_MODEL), pl.BlockSpec((1, N_MOD, D_MODEL), lambda b, j: (b, 0, 0))]
        + [_resident(a.shape) for a in consts],
        out_specs=out_specs,
        out_shape=out_shape,
        scratch_shapes=[pltpu.VMEM((8, LANES), F32), pltpu.VMEM((8, 2 * MLSTM_WIDTH), F32)],
        compiler_params=pltpu.CompilerParams(dimension_semantics=("arbitrary", "arbitrary"),
                                             vmem_limit_bytes=VMEM_LIMIT),
        name="pre",
    )(x, mod, *consts)


def _fox_kernel(q_ref, k_ref, vt_ref, o_ref, s_scr, p_scr, m_scr, alpha_scr, acc_scr, *, tq):
    qi = pl.program_id(2)
    heads = range(FOX_STEP_HEADS)
    lanes = [slice(FOX_GROUP * h, FOX_GROUP * (h + 1)) for h in heads]
    rows = [slice(FOX_HEAD_DIM * h, FOX_HEAD_DIM * (h + 1)) for h in heads]

    def put_scores(h, k0):
        s_scr[h] = lax.dot_general(k_ref[0, pl.ds(k0, tq), lanes[h]], q_ref[0, :, lanes[h]],
                                   (((1,), (1,)), ((), ())), preferred_element_type=F32)

    def softmax(h, masked):
        def block(r):
            s = s_scr[h, r:r + FOX_ROW_BLOCK, :]
            if masked:
                kpos = r + lax.broadcasted_iota(jnp.int32, (FOX_ROW_BLOCK, tq), 0)
                qpos = lax.broadcasted_iota(jnp.int32, (FOX_ROW_BLOCK, tq), 1)
                s = jnp.where(kpos <= qpos, s, -jnp.inf)
            return s

        top = block(0)
        for r in range(FOX_ROW_BLOCK, tq, FOX_ROW_BLOCK):
            top = jnp.maximum(top, block(r))
        m = m_scr[h, 0:1, :]
        m_new = jnp.maximum(m, jnp.max(top, axis=0, keepdims=True))
        alpha_scr[h, 0:1, :] = jnp.exp2(m - m_new)
        m_scr[h, 0:1, :] = m_new
        for r in range(0, tq, FOX_ROW_BLOCK):
            p_scr[h, r:r + FOX_ROW_BLOCK, :] = jnp.exp2(block(r) - m_new).astype(BF16)

    def weighted_values(h, k0):
        v_aug = jnp.concatenate([vt_ref[0, rows[h], pl.ds(k0, tq)], jnp.ones((16, tq), BF16)], axis=0)
        acc_scr[h] = alpha_scr[h, 0:1, :] * acc_scr[h] + jnp.dot(v_aug, p_scr[h], preferred_element_type=F32)

    def step(t, masked):
        k0 = pl.multiple_of(t * tq, tq)
        k_prev = pl.multiple_of(jnp.maximum(k0 - tq, 0), tq)
        for g in range(0, FOX_STEP_HEADS, FOX_ISSUE_HEADS):
            group = range(g, g + FOX_ISSUE_HEADS)
            for h in group:
                put_scores(h, k0)
            for h in group:
                weighted_values(h, k_prev)
            for h in group:
                softmax(h, masked)

    m_scr[...] = jnp.full(m_scr.shape, -jnp.inf, F32)
    alpha_scr[...] = jnp.ones(alpha_scr.shape, F32)
    acc_scr[...] = jnp.zeros(acc_scr.shape, F32)
    p_scr[...] = jnp.zeros(p_scr.shape, BF16)

    @pl.loop(0, qi)
    def _(t):
        step(t, False)

    step(qi, True)
    outs = []
    for h in heads:
        weighted_values(h, pl.multiple_of(qi * tq, tq))
        a = acc_scr[h]
        outs.append(a[0:FOX_HEAD_DIM] / a[FOX_HEAD_DIM:FOX_HEAD_DIM + 1])
    o_ref[0] = jnp.concatenate(outs, axis=0).T.astype(BF16)


def _fox(qa, ka, fvt):
    bsz, seq, _ = qa.shape
    tq, nh = FOX_TILE, FOX_STEP_HEADS
    return pl.pallas_call(
        functools.partial(_fox_kernel, tq=tq),
        grid=(bsz, FOX_HEADS // nh, seq // tq),
        in_specs=[pl.BlockSpec((1, tq, nh * FOX_GROUP), lambda b, g, i: (b, i, g)),
                  pl.BlockSpec((1, seq, nh * FOX_GROUP), lambda b, g, i: (b, 0, g)),
                  pl.BlockSpec((1, nh * FOX_HEAD_DIM, seq), lambda b, g, i: (b, g, 0))],
        out_specs=pl.BlockSpec((1, tq, nh * FOX_HEAD_DIM), lambda b, g, i: (b, i, g)),
        out_shape=jax.ShapeDtypeStruct((bsz, seq, FOX_WIDTH), BF16),
        scratch_shapes=[pltpu.VMEM((nh, tq, tq), F32), pltpu.VMEM((nh, tq, tq), BF16),
                        pltpu.VMEM((nh, 8, tq), F32), pltpu.VMEM((nh, 8, tq), F32),
                        pltpu.VMEM((nh, FOX_HEAD_DIM + 16, tq), F32)],
        compiler_params=pltpu.CompilerParams(dimension_semantics=("arbitrary",) * 3,
                                             vmem_limit_bytes=VMEM_LIMIT),
        name="fox",
    )(qa, ka, fvt)


def _mlstm_kernel(q_ref, k_ref, vt_ref, mo_ref, gtok_ref, gt_ref, gain_ref, *rest, ts, lm):
    n_w = (len(rest) - 3) // 2
    w_f32, (o_ref, *w_bf16), (c_scr, m_scr) = rest[:n_w], rest[n_w:2 * n_w + 1], rest[2 * n_w + 1:]
    for src, dst in zip(w_f32, w_bf16):
        dst[...] = src[...].astype(BF16)

    @pl.when(pl.program_id(1) == 0)
    def _():
        c_scr[...] = jnp.zeros_like(c_scr)
        m_scr[...] = jnp.zeros_like(m_scr)

    s_idx = lax.broadcasted_iota(jnp.int32, (lm, lm), 0)
    t_idx = lax.broadcasted_iota(jnp.int32, (lm, lm), 1)
    causal = s_idx <= t_idx
    dh = MLSTM_HEAD_DIM
    for c in range(ts // lm):
        tok = slice(c * lm, (c + 1) * lm)
        for h in range(MLSTM_HEADS):
            cols = slice(dh * h, dh * (h + 1))
            q = q_ref[0, tok, cols]
            k = k_ref[0, tok, cols]
            vt = vt_ref[0, cols, tok]
            a_row = gt_ref[0, h:h + 1, tok]
            u_row = gt_ref[0, 8 + h:9 + h, tok]
            u_col = gtok_ref[0, tok, MLSTM_GATE_LANE + h:MLSTM_GATE_LANE + h + 1]
            m_prev = m_scr[h:h + 1, 0:1]
            c_prev = c_scr[h]

            g = a_row[:, lm - 1:lm]
            w_end = g + u_row
            m_new = jnp.maximum(g + m_prev, jnp.max(w_end, axis=1, keepdims=True))
            decay = jnp.exp(g + m_prev - m_new)
            wt = jnp.exp(w_end - m_new)

            d_log = jnp.where(causal, a_row + u_col, -jnp.inf)
            inter = a_row + m_prev
            m_t = jnp.maximum(inter, jnp.max(d_log, axis=0, keepdims=True))
            p = jnp.exp(d_log - m_t)
            inter_w = jnp.exp(inter - m_t)

            scores = lax.dot_general(k, q, (((1,), (1,)), ((), ())), preferred_element_type=F32) * p
            cq = lax.dot_general(c_prev.astype(BF16), q, (((1,), (1,)), ((), ())),
                                 preferred_element_type=F32)
            num = jnp.dot(vt, scores.astype(BF16), preferred_element_type=F32) + inter_w * cq[0:dh]
            den = jnp.sum(scores, axis=0, keepdims=True) + inter_w * cq[dh:dh + 1]
            out_t = num / jnp.maximum(jnp.abs(den), jnp.exp(-m_t))
            hn = out_t * lax.rsqrt(jnp.mean(out_t * out_t, axis=0, keepdims=True) + EPS)
            hb = hn.T * gain_ref[:, cols]
            o_ref[0, tok, cols] = (jax.nn.sigmoid(mo_ref[0, tok, cols]) * hb).astype(BF16)

            v_w = jnp.concatenate([vt.astype(F32) * wt, jnp.broadcast_to(wt, (16, lm))], axis=0).astype(BF16)
            c_scr[h] = decay * c_prev + jnp.dot(v_w, k, preferred_element_type=F32)
            m_scr[h:h + 1, :] = jnp.broadcast_to(m_new, (1, LANES))


def _mlstm(mqk, mvt, mo, gtok, gt, gain, post_weights):
    bsz, seq, _ = mo.shape
    ts, lm = MLSTM_SEQ_TILE, MLSTM_CHUNK
    steps = bsz * (seq // ts)
    assert all(w.shape[0] % (16 * steps) == 0 for w in post_weights), "weight rows per grid step: whole bf16 tiles"
    share = lambda w: pl.BlockSpec((w.shape[0] // steps, w.shape[1]), lambda b, i: (b * (seq // ts) + i, 0))
    out = pl.pallas_call(
        functools.partial(_mlstm_kernel, ts=ts, lm=lm),
        grid=(bsz, seq // ts),
        in_specs=[pl.BlockSpec((1, ts, MLSTM_WIDTH), lambda b, i: (b, i, 0)),
                  pl.BlockSpec((1, ts, MLSTM_WIDTH), lambda b, i: (b, i, 1)),
                  pl.BlockSpec((1, MLSTM_WIDTH, ts), lambda b, i: (b, 0, i)),
                  pl.BlockSpec((1, ts, MLSTM_WIDTH), lambda b, i: (b, i, 0)),
                  pl.BlockSpec((1, ts, LANES), lambda b, i: (b, i, 0)),
                  pl.BlockSpec((1, 16, ts), lambda b, i: (b, 0, i)),
                  pl.BlockSpec((1, MLSTM_WIDTH), lambda b, i: (0, 0))] + [share(w) for w in post_weights],
        out_specs=[pl.BlockSpec((1, ts, MLSTM_WIDTH), lambda b, i: (b, i, 0))] + [share(w) for w in post_weights],
        out_shape=[jax.ShapeDtypeStruct((bsz, seq, MLSTM_WIDTH), BF16)]
        + [jax.ShapeDtypeStruct(w.shape, BF16) for w in post_weights],
        scratch_shapes=[pltpu.VMEM((MLSTM_HEADS, MLSTM_STATE_ROWS, MLSTM_HEAD_DIM), F32),
                        pltpu.VMEM((8, LANES), F32)],
        compiler_params=pltpu.CompilerParams(dimension_semantics=("arbitrary", "arbitrary"),
                                             vmem_limit_bytes=VMEM_LIMIT),
        name="mlstm",
    )(mqk, mqk, mvt, mo, gtok, gt, gain, *post_weights)
    return out[0], out[1:]


def _post_kernel(x1_ref, ya_ref, yb_ref, mod_ref, wo_ref, w2i_ref, w2o_ref, gf_ref, o_ref, x2_scr):
    gt2, sh3 = _rows(mod_ref[0, 5:6, :]), _rows(mod_ref[0, 6:7, :])
    sc3p, gt3h, gain = _rows(1 + mod_ref[0, 7:8, :]), _rows(0.5 * mod_ref[0, 8:9, :]), _rows(gf_ref[...])
    tm = x1_ref.shape[1]
    ya = jnp.dot(ya_ref[0], wo_ref[0:FOX_WIDTH, :], preferred_element_type=F32)
    yb = jnp.dot(yb_ref[0], wo_ref[FOX_WIDTH:, :], preferred_element_type=F32)

    def residual_and_norm(r):
        x2 = x1_ref[0, r, :] + gt2 * (ya[r] + yb[r])
        x2_scr[r, :] = x2
        return (_rms(x2) * sc3p + sh3).astype(BF16)

    h = _by_rows(residual_and_norm, tm, VEC_ROWS)
    ffn2 = _ffn(h, w2i_ref, w2o_ref)
    for r in range(0, tm, VEC_ROWS):
        x3 = x2_scr[r:r + VEC_ROWS, :] + gt3h * ffn2[r:r + VEC_ROWS]
        o_ref[0, r:r + VEC_ROWS, :] = _rms(x3) * gain


def _post(x1, ya, yb, mod, wo, w2i, w2o, gf):
    bsz, seq, _ = x1.shape
    tm = ROW_TILE
    tok = lambda width: pl.BlockSpec((1, tm, width), lambda b, j: (b, j, 0))
    consts = (wo, w2i, w2o, gf)
    return pl.pallas_call(
        _post_kernel,
        grid=(bsz, seq // tm),
        in_specs=[tok(D_MODEL), tok(FOX_WIDTH), tok(MLSTM_WIDTH),
                  pl.BlockSpec((1, N_MOD, D_MODEL), lambda b, j: (b, 0, 0))]
        + [_resident(a.shape) for a in consts],
        out_specs=tok(D_MODEL),
        out_shape=jax.ShapeDtypeStruct((bsz, seq, D_MODEL), F32),
        scratch_shapes=[pltpu.VMEM((tm, D_MODEL), F32)],
        compiler_params=pltpu.CompilerParams(dimension_semantics=("arbitrary", "arbitrary"),
                                             vmem_limit_bytes=VMEM_LIMIT),
        name="post",
    )(x1, ya, yb, mod, *consts)


def _gate_placement():
    pq = np.zeros((LANES, FOX_HEADS * FOX_GROUP), np.float32)
    pk = np.zeros((LANES, FOX_HEADS * FOX_GROUP), np.float32)
    cq = np.zeros((1, FOX_HEADS * FOX_GROUP), np.float32)
    ck = np.zeros((1, FOX_HEADS * FOX_GROUP), np.float32)
    for h in range(FOX_HEADS):
        base = FOX_GROUP * h + GATE_LANE
        for part in range(3):
            pq[8 * part + h, base + part] = 1.0
            pk[8 * part + h, base + 3 + part] = -1.0
            cq[0, base + 3 + part] = 1.0
            ck[0, base + part] = 1.0
    return jnp.asarray(pq, BF16), jnp.asarray(pk, BF16), jnp.asarray(cq), jnp.asarray(ck)


def kernel(x, c, w_ada, b_ada, w_ffn1_in, w_ffn1_out, w_mix_in, b_fox_f, b_m_i, b_m_f, w_conv, b_conv, g_mhn,
           w_mix_out, w_ffn2_in, w_ffn2_out, g_final):
    assert w_ada.shape[0] == 1, "one layer"
    bsz, seq, _ = x.shape
    assert seq % max(ROW_TILE, MLSTM_SEQ_TILE, FOX_TILE) == 0

    mod, (w1_in, w1_out) = _ada(c, w_ada[0], b_ada[0], (w_ffn1_in[0], w_ffn1_out[0]))
    mod = mod.reshape(bsz, N_MOD, D_MODEL)

    wfox, wml, wg = _split_mix(jnp.swapaxes(w_mix_in, 1, 2)[0])
    zrow = lambda n: jnp.zeros((n,), F32)
    gbias = jnp.concatenate([b_fox_f[0], b_fox_f[0], b_fox_f[0], b_m_i[0], zrow(LANES - 28),
                             zrow(MLSTM_GATE_LANE), b_m_f[0], zrow(LANES - 28)]).reshape(1, 2 * LANES)
    pq, pk, cq, ck = _gate_placement()
    kscale = jnp.concatenate([jnp.ones((MLSTM_WIDTH,), F32),
                              jnp.full((MLSTM_WIDTH,), MLSTM_HEAD_DIM ** -0.5, F32)]).reshape(1, 2 * MLSTM_WIDTH)

    x1, qa, ka, fvt, mqk, mvt, mo, gtok, gt = _pre(
        x, mod, w1_in, w1_out, wfox, wml, wg, gbias,
        pq, pk, cq, ck, w_conv[0], b_conv[0].reshape(1, -1), kscale)

    ya = _fox(qa, ka, fvt)
    yb, (w_out, w2_in, w2_out) = _mlstm(mqk, mvt, mo, gtok, gt, g_mhn[0].reshape(1, MLSTM_WIDTH),
                                        (w_mix_out[0], w_ffn2_in[0], w_ffn2_out[0]))
    return _post(x1, ya, yb, mod, w_out, w2_in, w2_out, g_final.reshape(1, D_MODEL))
```

```python
import functools

import jax
import jax.numpy as jnp
import numpy as np
from jax import lax
from jax.experimental import pallas as pl
from jax.experimental.pallas import tpu as pltpu

F32 = jnp.float32
BF16 = jnp.bfloat16

D_MODEL = 1024
D_FF = 2816
N_MOD = 9
EPS = 1e-6
LOG2_E = 1.4426950408889634
FOX_HEADS = 8
FOX_HEAD_DIM = 64
FOX_WIDTH = FOX_HEADS * FOX_HEAD_DIM
MLSTM_HEADS = 4
MLSTM_HEAD_DIM = 128
MLSTM_WIDTH = MLSTM_HEADS * MLSTM_HEAD_DIM
CONV_WIDTH = 4

LANES = 128
VEC_ROWS = 16
FOX_GROUP = 128
GATE_LANE = FOX_HEAD_DIM
MLSTM_GATE_LANE = 24
MLSTM_STATE_ROWS = MLSTM_HEAD_DIM + 16

ADA_STEPS = 8
ROW_TILE = 512
FFN_CHUNKS = (512, 512, 512, 512, 512, 256)
MLSTM_CHUNK = 256
MLSTM_SEQ_TILE = 1024
FOX_TILE = 512
FOX_STEP_HEADS = 8
FOX_ISSUE_HEADS = 2
FOX_ROW_BLOCK = 32
VMEM_LIMIT = 56 * 1024 * 1024


def _rms(x):
    return x * lax.rsqrt(jnp.mean(x * x, axis=-1, keepdims=True) + EPS)


def _by_rows(fn, rows, step):
    return jnp.concatenate([fn(slice(r, r + step)) for r in range(0, rows, step)], axis=0)


def _rows(row):
    return jnp.broadcast_to(row, (VEC_ROWS, row.shape[-1]))


def _split3(x):
    hi = x.astype(BF16)
    r1 = x - hi.astype(F32)
    mid = r1.astype(BF16)
    lo = (r1 - mid.astype(F32)).astype(BF16)
    return hi, mid, lo


def _ffn(h, w_in_ref, w_out_ref):
    acc = None
    j0 = 0
    for width in FFN_CHUNKS:
        g = jnp.dot(h, w_in_ref[:, j0:j0 + width], preferred_element_type=F32)
        u = jnp.dot(h, w_in_ref[:, D_FF + j0:D_FF + j0 + width], preferred_element_type=F32)
        a = _by_rows(lambda r: (jax.nn.silu(g[r]) * u[r]).astype(BF16), g.shape[0], 2 * VEC_ROWS)
        part = jnp.dot(a, w_out_ref[j0:j0 + width, :], preferred_element_type=F32)
        acc = part if acc is None else acc + part
        j0 += width
    return acc


def _resident(shape):
    zeros = (0,) * len(shape)
    return pl.BlockSpec(shape, lambda *_: zeros, pipeline_mode=pl.Buffered(1))


def _ada_kernel(c_ref, w_ref, b_ref, *rest):
    n_w = (len(rest) - 1) // 2
    w_f32, (o_ref, *w_bf16) = rest[:n_w], rest[n_w:]
    for src, dst in zip(w_f32, w_bf16):
        dst[...] = src[...].astype(BF16)
    c_act = jax.nn.silu(c_ref[...])
    o_ref[...] = jnp.dot(c_act, w_ref[...], preferred_element_type=F32,
                         precision=lax.Precision.HIGHEST) + b_ref[...]


def _ada(c, w, b, pre_weights):
    rows = c.shape[0]
    n = w.shape[1]
    steps = ADA_STEPS
    tn = n // steps
    assert all(p.shape[0] % (16 * steps) == 0 for p in pre_weights), "weight rows per grid step: whole bf16 tiles"
    share = lambda p: pl.BlockSpec((p.shape[0] // steps, p.shape[1]), lambda j: (j, 0))
    out = pl.pallas_call(
        _ada_kernel,
        grid=(steps,),
        in_specs=[pl.BlockSpec((rows, D_MODEL), lambda j: (0, 0)),
                  pl.BlockSpec((D_MODEL, tn), lambda j: (0, j)),
                  pl.BlockSpec((1, tn), lambda j: (0, j))] + [share(p) for p in pre_weights],
        out_specs=[pl.BlockSpec((rows, tn), lambda j: (0, j))] + [share(p) for p in pre_weights],
        out_shape=[jax.ShapeDtypeStruct((rows, n), F32)] + [jax.ShapeDtypeStruct(p.shape, BF16) for p in pre_weights],
        compiler_params=pltpu.CompilerParams(dimension_semantics=("arbitrary",),
                                             vmem_limit_bytes=VMEM_LIMIT),
        name="ada",
    )(c, w, b.reshape(1, n), *pre_weights)
    return out[0], out[1:]


MIX_SEGMENT = 512


def _split_mix_kernel(wt_ref, wfox_ref, wml_ref, wg_ref):
    q_scale = FOX_HEAD_DIM ** -0.5 * LOG2_E
    for c in range(0, 3 * FOX_WIDTH, MIX_SEGMENT):
        blk = wt_ref[c:c + MIX_SEGMENT, :].T
        wfox_ref[:, c:c + MIX_SEGMENT] = (blk * q_scale if c < FOX_WIDTH else blk).astype(BF16)
    first = 3 * FOX_WIDTH + FOX_HEADS
    for c in range(0, 4 * MLSTM_WIDTH, MIX_SEGMENT):
        wml_ref[:, c:c + MIX_SEGMENT] = wt_ref[first + c:first + c + MIX_SEGMENT, :].T.astype(BF16)
    rows = jnp.concatenate([wt_ref[3 * FOX_WIDTH:first, :], wt_ref[first + 4 * MLSTM_WIDTH:, :],
                            jnp.zeros((LANES - FOX_HEADS - 2 * MLSTM_HEADS, wt_ref.shape[1]), F32)], axis=0)
    cols = rows.T
    ff, mi = cols[:, 0:FOX_HEADS], cols[:, FOX_HEADS:FOX_HEADS + MLSTM_HEADS]
    mf = cols[:, FOX_HEADS + MLSTM_HEADS:FOX_HEADS + 2 * MLSTM_HEADS]
    zeros = lambda n: jnp.zeros((cols.shape[0], n), F32)
    wg_ref[...] = jnp.concatenate([ff, ff, ff, mi, zeros(LANES - 28), zeros(MLSTM_GATE_LANE), mf, zeros(LANES - 28)],
                                  axis=1).astype(BF16)


def _split_mix(w_mix_t):
    n, d = w_mix_t.shape
    return pl.pallas_call(
        _split_mix_kernel,
        out_shape=[jax.ShapeDtypeStruct((d, 3 * FOX_WIDTH), BF16), jax.ShapeDtypeStruct((d, 4 * MLSTM_WIDTH), BF16),
                   jax.ShapeDtypeStruct((d, 2 * LANES), BF16)],
        compiler_params=pltpu.CompilerParams(vmem_limit_bytes=VMEM_LIMIT),
        name="split_mix",
    )(w_mix_t)


def _pre_kernel(x_ref, mod_ref, w1i_ref, w1o_ref, wfox_ref, wml_ref,
                wg_ref, gbias_ref, pq_ref, pk_ref, cq_ref, ck_ref, convw_ref, convb_ref, kscale_ref,
                x1_ref, qa_ref, ka_ref, fvt_ref, mqk_ref, mvt_ref, mo_ref, gtok_ref, gt_ref,
                cum_carry, conv_tail, *, tm, lm):
    hm = tm // 2
    sh1, sc1p, gt1h = _rows(mod_ref[0, 0:1, :]), _rows(1 + mod_ref[0, 1:2, :]), _rows(0.5 * mod_ref[0, 2:3, :])
    sh2, sc2p = _rows(mod_ref[0, 3:4, :]), _rows(1 + mod_ref[0, 4:5, :])

    @pl.when(pl.program_id(1) == 0)
    def _():
        cum_carry[...] = jnp.zeros_like(cum_carry)
        conv_tail[...] = jnp.zeros_like(conv_tail)

    row = lax.broadcasted_iota(jnp.int32, (hm, hm), 0)
    col = lax.broadcasted_iota(jnp.int32, (hm, hm), 1)
    dist = row - col
    tri_seq = jnp.where(dist >= 0, 1.0, 0.0).astype(BF16)
    tri_chunk = jnp.where(dist >= 0, jnp.where(dist <= (row & (lm - 1)), 1.0, 0.0), 0.0).astype(BF16)
    lane = lax.broadcasted_iota(jnp.int32, (hm, LANES), 1)
    head_lane = lax.broadcasted_iota(jnp.int32, (VEC_ROWS, FOX_GROUP), 1) < FOX_HEAD_DIM
    q_ones, k_ones = _rows(cq_ref[...]), _rows(ck_ref[...])

    def cumsum_rows(tri, v):
        hi, mid, lo = _split3(v)
        r = jnp.dot(tri, jnp.concatenate([hi, mid, lo], axis=1), preferred_element_type=F32)
        return (r[:, :LANES] + r[:, LANES:2 * LANES]) + r[:, 2 * LANES:]

    def fox_groups(proj, gate, ones, r):
        groups = []
        for pair in range(FOX_HEADS // 2):
            both = proj[r, FOX_GROUP * pair:FOX_GROUP * (pair + 1)]
            for head, vals in ((2 * pair, both), (2 * pair + 1, pltpu.roll(both, FOX_HEAD_DIM, axis=1))):
                lanes = slice(FOX_GROUP * head, FOX_GROUP * (head + 1))
                groups.append(jnp.where(head_lane, vals, gate[r, lanes] + ones[:, lanes]))
        return jnp.concatenate(groups, axis=1).astype(BF16)

    def norm1(base):
        return _by_rows(lambda r: (_rms(x_ref[0, base + r.start:base + r.stop, :]) * sc1p + sh1).astype(BF16),
                        hm, VEC_ROWS)

    def residual_and_norm(base, ffn):
        def block(r):
            rows = slice(base + r.start, base + r.stop)
            x1 = x_ref[0, rows, :] + gt1h * ffn[r]
            x1_ref[0, rows, :] = x1
            return (_rms(x1) * sc2p + sh2).astype(BF16)
        return _by_rows(block, hm, VEC_ROWS)

    def mix(base, h2, cum_in, tail):
        rows = slice(base, base + hm)
        z = jnp.dot(h2, wg_ref[...], preferred_element_type=F32) + gbias_ref[...]
        mqk = jnp.dot(h2, wml_ref[:, 0:2 * MLSTM_WIDTH], preferred_element_type=F32)
        za, zb = z[:, :LANES], z[:, LANES:]
        fvt_ref[0, :, rows] = jnp.dot(h2, wfox_ref[:, 2 * FOX_WIDTH:], preferred_element_type=F32).T.astype(BF16)

        cum_f = cumsum_rows(tri_seq, jax.nn.log_sigmoid(za)) + cum_in
        a_loc = cumsum_rows(tri_chunk, jax.nn.log_sigmoid(zb))
        u_gate = za - a_loc
        gtok_ref[0, rows, :] = u_gate
        gt_ref[0, 0:8, rows] = a_loc.T[MLSTM_GATE_LANE:MLSTM_GATE_LANE + 8, :]
        gt_ref[0, 8:16, rows] = u_gate.T[MLSTM_GATE_LANE:MLSTM_GATE_LANE + 8, :]
        cum_2 = cum_f * LOG2_E
        c_hi = cum_2.astype(BF16).astype(F32)
        c_mid = (cum_2 - c_hi).astype(BF16).astype(F32)
        c_lo = (cum_2 - c_hi) - c_mid
        packed = jnp.where(lane < 8, c_hi, jnp.where(lane < 16, c_mid, jnp.where(lane < 24, c_lo, 0.0))).astype(BF16)

        mvt_ref[0, :, rows] = jnp.dot(h2, wml_ref[:, 2 * MLSTM_WIDTH:3 * MLSTM_WIDTH], preferred_element_type=F32).T.astype(BF16)

        ext = jnp.concatenate([tail, mqk], axis=0)
        for c in range(0, 2 * MLSTM_WIDTH, MLSTM_WIDTH):
            cols = slice(c, c + MLSTM_WIDTH)
            taps = [_rows(convw_ref[t:t + 1, cols]) for t in range(CONV_WIDTH)]
            bias, out_scale = _rows(convb_ref[:, cols]), _rows(kscale_ref[:, cols])
            for r in range(0, hm, VEC_ROWS):
                y = bias + taps[3] * ext[8 + r:8 + r + VEC_ROWS, cols]
                y = y + taps[2] * ext[7 + r:7 + r + VEC_ROWS, cols]
                y = y + taps[1] * ext[6 + r:6 + r + VEC_ROWS, cols]
                y = y + taps[0] * ext[5 + r:5 + r + VEC_ROWS, cols]
                mqk_ref[0, base + r:base + r + VEC_ROWS, cols] = (jax.nn.silu(y) * out_scale).astype(BF16)

        mo_ref[0, rows, :] = jnp.dot(h2, wml_ref[:, 3 * MLSTM_WIDTH:], preferred_element_type=F32)
        q = jnp.dot(h2, wfox_ref[:, 0:FOX_WIDTH], preferred_element_type=F32)
        q_gate = jnp.dot(packed, pq_ref[...], preferred_element_type=F32)
        for r in range(0, hm, VEC_ROWS):
            qa_ref[0, base + r:base + r + VEC_ROWS, :] = fox_groups(q, q_gate, q_ones, slice(r, r + VEC_ROWS))
        k = jnp.dot(h2, wfox_ref[:, FOX_WIDTH:2 * FOX_WIDTH], preferred_element_type=F32)
        k_gate = jnp.dot(packed, pk_ref[...], preferred_element_type=F32)
        for r in range(0, hm, VEC_ROWS):
            ka_ref[0, base + r:base + r + VEC_ROWS, :] = fox_groups(k, k_gate, k_ones, slice(r, r + VEC_ROWS))
        return cum_f[hm - 1:hm, :], mqk[hm - 8:hm, :]

    ffn = _ffn(jnp.concatenate([norm1(0), norm1(hm)], axis=0), w1i_ref, w1o_ref)
    h2_a = residual_and_norm(0, ffn[0:hm])
    h2_b = residual_and_norm(hm, ffn[hm:])
    cum_a, tail_a = mix(0, h2_a, cum_carry[0:1, :], conv_tail[...])
    cum_b, tail_b = mix(hm, h2_b, cum_a, tail_a)
    cum_carry[0:1, :] = cum_b
    conv_tail[...] = tail_b


def _pre(x, mod, w1i, w1o, wfox, wml, wg, gbias, pq, pk, cq, ck, convw, convb, kscale):
    bsz, seq, _ = x.shape
    tm = ROW_TILE
    grid = (bsz, seq // tm)
    tok = lambda width: pl.BlockSpec((1, tm, width), lambda b, j: (b, j, 0))
    feat = lambda rows: pl.BlockSpec((1, rows, tm), lambda b, j: (b, 0, j))
    consts = (w1i, w1o, wfox, wml, wg, gbias, pq, pk, cq, ck, convw, convb, kscale)
    out_shape = (
        jax.ShapeDtypeStruct((bsz, seq, D_MODEL), F32),
        jax.ShapeDtypeStruct((bsz, seq, FOX_HEADS * FOX_GROUP), BF16),
        jax.ShapeDtypeStruct((bsz, seq, FOX_HEADS * FOX_GROUP), BF16),
        jax.ShapeDtypeStruct((bsz, FOX_WIDTH, seq), BF16),
        jax.ShapeDtypeStruct((bsz, seq, 2 * MLSTM_WIDTH), BF16),
        jax.ShapeDtypeStruct((bsz, MLSTM_WIDTH, seq), BF16),
        jax.ShapeDtypeStruct((bsz, seq, MLSTM_WIDTH), F32),
        jax.ShapeDtypeStruct((bsz, seq, LANES), F32),
        jax.ShapeDtypeStruct((bsz, 16, seq), F32),
    )
    out_specs = (tok(D_MODEL), tok(FOX_HEADS * FOX_GROUP), tok(FOX_HEADS * FOX_GROUP), feat(FOX_WIDTH),
                 tok(2 * MLSTM_WIDTH), feat(MLSTM_WIDTH), tok(MLSTM_WIDTH), tok(LANES), feat(16))
    return pl.pallas_call(
        functools.partial(_pre_kernel, tm=tm, lm=MLSTM_CHUNK),
        grid=grid,
        in_specs=[tok(D_MODEL), pl.BlockSpec((1, N_MOD, D_MODEL), lambda b, j: (b, 0, 0))]
        + [_resident(a.shape) for a in consts],
        out_specs=out_specs,
        out_shape=out_shape,
        scratch_shapes=[pltpu.VMEM((8, LANES), F32), pltpu.VMEM((8, 2 * MLSTM_WIDTH), F32)],
        compiler_params=pltpu.CompilerParams(dimension_semantics=("arbitrary", "arbitrary"),
                                             vmem_limit_bytes=VMEM_LIMIT),
        name="pre",
    )(x, mod, *consts)


def _fox_kernel(q_ref, k_ref, vt_ref, o_ref, s_scr, p_scr, m_scr, alpha_scr, acc_scr, *, tq):
    qi = pl.program_id(2)
    heads = range(FOX_STEP_HEADS)
    lanes = [slice(FOX_GROUP * h, FOX_GROUP * (h + 1)) for h in heads]
    rows = [slice(FOX_HEAD_DIM * h, FOX_HEAD_DIM * (h + 1)) for h in heads]

    def put_scores(h, k0):
        s_scr[h] = lax.dot_general(k_ref[0, pl.ds(k0, tq), lanes[h]], q_ref[0, :, lanes[h]],
                                   (((1,), (1,)), ((), ())), preferred_element_type=F32)

    def softmax(h, masked):
        def block(r):
            s = s_scr[h, r:r + FOX_ROW_BLOCK, :]
            if masked:
                kpos = r + lax.broadcasted_iota(jnp.int32, (FOX_ROW_BLOCK, tq), 0)
                qpos = lax.broadcasted_iota(jnp.int32, (FOX_ROW_BLOCK, tq), 1)
                s = jnp.where(kpos <= qpos, s, -jnp.inf)
            return s

        top = block(0)
        for r in range(FOX_ROW_BLOCK, tq, FOX_ROW_BLOCK):
            top = jnp.maximum(top, block(r))
        m = m_scr[h, 0:1, :]
        m_new = jnp.maximum(m, jnp.max(top, axis=0, keepdims=True))
        alpha_scr[h, 0:1, :] = jnp.exp2(m - m_new)
        m_scr[h, 0:1, :] = m_new
        for r in range(0, tq, FOX_ROW_BLOCK):
            p_scr[h, r:r + FOX_ROW_BLOCK, :] = jnp.exp2(block(r) - m_new).astype(BF16)

    def weighted_values(h, k0):
        v_aug = jnp.concatenate([vt_ref[0, rows[h], pl.ds(k0, tq)], jnp.ones((16, tq), BF16)], axis=0)
        acc_scr[h] = alpha_scr[h, 0:1, :] * acc_scr[h] + jnp.dot(v_aug, p_scr[h], preferred_element_type=F32)

    def step(t, masked):
        k0 = pl.multiple_of(t * tq, tq)
        k_prev = pl.multiple_of(jnp.maximum(k0 - tq, 0), tq)
        for g in range(0, FOX_STEP_HEADS, FOX_ISSUE_HEADS):
            group = range(g, g + FOX_ISSUE_HEADS)
            for h in group:
                put_scores(h, k0)
            for h in group:
                weighted_values(h, k_prev)
            for h in group:
                softmax(h, masked)

    m_scr[...] = jnp.full(m_scr.shape, -jnp.inf, F32)
    alpha_scr[...] = jnp.ones(alpha_scr.shape, F32)
    acc_scr[...] = jnp.zeros(acc_scr.shape, F32)
    p_scr[...] = jnp.zeros(p_scr.shape, BF16)

    @pl.loop(0, qi)
    def _(t):
        step(t, False)

    step(qi, True)
    outs = []
    for h in heads:
        weighted_values(h, pl.multiple_of(qi * tq, tq))
        a = acc_scr[h]
        outs.append(a[0:FOX_HEAD_DIM] / a[FOX_HEAD_DIM:FOX_HEAD_DIM + 1])
    o_ref[0] = jnp.concatenate(outs, axis=0).T.astype(BF16)


def _fox(qa, ka, fvt):
    bsz, seq, _ = qa.shape
    tq, nh = FOX_TILE, FOX_STEP_HEADS
    return pl.pallas_call(
        functools.partial(_fox_kernel, tq=tq),
        grid=(bsz, FOX_HEADS // nh, seq // tq),
        in_specs=[pl.BlockSpec((1, tq, nh * FOX_GROUP), lambda b, g, i: (b, i, g)),
                  pl.BlockSpec((1, seq, nh * FOX_GROUP), lambda b, g, i: (b, 0, g)),
                  pl.BlockSpec((1, nh * FOX_HEAD_DIM, seq), lambda b, g, i: (b, g, 0))],
        out_specs=pl.BlockSpec((1, tq, nh * FOX_HEAD_DIM), lambda b, g, i: (b, i, g)),
        out_shape=jax.ShapeDtypeStruct((bsz, seq, FOX_WIDTH), BF16),
        scratch_shapes=[pltpu.VMEM((nh, tq, tq), F32), pltpu.VMEM((nh, tq, tq), BF16),
                        pltpu.VMEM((nh, 8, tq), F32), pltpu.VMEM((nh, 8, tq), F32),
                        pltpu.VMEM((nh, FOX_HEAD_DIM + 16, tq), F32)],
        compiler_params=pltpu.CompilerParams(dimension_semantics=("arbitrary",) * 3,
                                             vmem_limit_bytes=VMEM_LIMIT),
        name="fox",
    )(qa, ka, fvt)


def _mlstm_kernel(q_ref, k_ref, vt_ref, mo_ref, gtok_ref, gt_ref, gain_ref, *rest, ts, lm):
    n_w = (len(rest) - 3) // 2
    w_f32, (o_ref, *w_bf16), (c_scr, m_scr) = rest[:n_w], rest[n_w:2 * n_w + 1], rest[2 * n_w + 1:]
    for src, dst in zip(w_f32, w_bf16):
        dst[...] = src[...].astype(BF16)

    @pl.when(pl.program_id(1) == 0)
    def _():
        c_scr[...] = jnp.zeros_like(c_scr)
        m_scr[...] = jnp.zeros_like(m_scr)

    s_idx = lax.broadcasted_iota(jnp.int32, (lm, lm), 0)
    t_idx = lax.broadcasted_iota(jnp.int32, (lm, lm), 1)
    causal = s_idx <= t_idx
    dh = MLSTM_HEAD_DIM
    for c in range(ts // lm):
        tok = slice(c * lm, (c + 1) * lm)
        for h in range(MLSTM_HEADS):
            cols = slice(dh * h, dh * (h + 1))
            q = q_ref[0, tok, cols]
            k = k_ref[0, tok, cols]
            vt = vt_ref[0, cols, tok]
            a_row = gt_ref[0, h:h + 1, tok]
            u_row = gt_ref[0, 8 + h:9 + h, tok]
            u_col = gtok_ref[0, tok, MLSTM_GATE_LANE + h:MLSTM_GATE_LANE + h + 1]
            m_prev = m_scr[h:h + 1, 0:1]
            c_prev = c_scr[h]

            g = a_row[:, lm - 1:lm]
            w_end = g + u_row
            m_new = jnp.maximum(g + m_prev, jnp.max(w_end, axis=1, keepdims=True))
            decay = jnp.exp(g + m_prev - m_new)
            wt = jnp.exp(w_end - m_new)

            d_log = jnp.where(causal, a_row + u_col, -jnp.inf)
            inter = a_row + m_prev
            m_t = jnp.maximum(inter, jnp.max(d_log, axis=0, keepdims=True))
            p = jnp.exp(d_log - m_t)
            inter_w = jnp.exp(inter - m_t)

            scores = lax.dot_general(k, q, (((1,), (1,)), ((), ())), preferred_element_type=F32) * p
            cq = lax.dot_general(c_prev.astype(BF16), q, (((1,), (1,)), ((), ())),
                                 preferred_element_type=F32)
            num = jnp.dot(vt, scores.astype(BF16), preferred_element_type=F32) + inter_w * cq[0:dh]
            den = jnp.sum(scores, axis=0, keepdims=True) + inter_w * cq[dh:dh + 1]
            out_t = num / jnp.maximum(jnp.abs(den), jnp.exp(-m_t))
            hn = out_t * lax.rsqrt(jnp.mean(out_t * out_t, axis=0, keepdims=True) + EPS)
            hb = hn.T * gain_ref[:, cols]
            o_ref[0, tok, cols] = (jax.nn.sigmoid(mo_ref[0, tok, cols]) * hb).astype(BF16)

            v_w = jnp.concatenate([vt.astype(F32) * wt, jnp.broadcast_to(wt, (16, lm))], axis=0).astype(BF16)
            c_scr[h] = decay * c_prev + jnp.dot(v_w, k, preferred_element_type=F32)
            m_scr[h:h + 1, :] = jnp.broadcast_to(m_new, (1, LANES))


def _mlstm(mqk, mvt, mo, gtok, gt, gain, post_weights):
    bsz, seq, _ = mo.shape
    ts, lm = MLSTM_SEQ_TILE, MLSTM_CHUNK
    steps = bsz * (seq // ts)
    assert all(w.shape[0] % (16 * steps) == 0 for w in post_weights), "weight rows per grid step: whole bf16 tiles"
    share = lambda w: pl.BlockSpec((w.shape[0] // steps, w.shape[1]), lambda b, i: (b * (seq // ts) + i, 0))
    out = pl.pallas_call(
        functools.partial(_mlstm_kernel, ts=ts, lm=lm),
        grid=(bsz, seq // ts),
        in_specs=[pl.BlockSpec((1, ts, MLSTM_WIDTH), lambda b, i: (b, i, 0)),
                  pl.BlockSpec((1, ts, MLSTM_WIDTH), lambda b, i: (b, i, 1)),
                  pl.BlockSpec((1, MLSTM_WIDTH, ts), lambda b, i: (b, 0, i)),
                  pl.BlockSpec((1, ts, MLSTM_WIDTH), lambda b, i: (b, i, 0)),
                  pl.BlockSpec((1, ts, LANES), lambda b, i: (b, i, 0)),
                  pl.BlockSpec((1, 16, ts), lambda b, i: (b, 0, i)),
                  pl.BlockSpec((1, MLSTM_WIDTH), lambda b, i: (0, 0))] + [share(w) for w in post_weights],
        out_specs=[pl.BlockSpec((1, ts, MLSTM_WIDTH), lambda b, i: (b, i, 0))] + [share(w) for w in post_weights],
        out_shape=[jax.ShapeDtypeStruct((bsz, seq, MLSTM_WIDTH), BF16)]
        + [jax.ShapeDtypeStruct(w.shape, BF16) for w in post_weights],
        scratch_shapes=[pltpu.VMEM((MLSTM_HEADS, MLSTM_STATE_ROWS, MLSTM_HEAD_DIM), F32),
                        pltpu.VMEM((8, LANES), F32)],
        compiler_params=pltpu.CompilerParams(dimension_semantics=("arbitrary", "arbitrary"),
                                             vmem_limit_bytes=VMEM_LIMIT),
        name="mlstm",
    )(mqk, mqk, mvt, mo, gtok, gt, gain, *post_weights)
    return out[0], out[1:]


def _post_kernel(x1_ref, ya_ref, yb_ref, mod_ref, wo_ref, w2i_ref, w2o_ref, gf_ref, o_ref, x2_scr):
    gt2, sh3 = _rows(mod_ref[0, 5:6, :]), _rows(mod_ref[0, 6:7, :])
    sc3p, gt3h, gain = _rows(1 + mod_ref[0, 7:8, :]), _rows(0.5 * mod_ref[0, 8:9, :]), _rows(gf_ref[...])
    tm = x1_ref.shape[1]
    ya = jnp.dot(ya_ref[0], wo_ref[0:FOX_WIDTH, :], preferred_element_type=F32)
    yb = jnp.dot(yb_ref[0], wo_ref[FOX_WIDTH:, :], preferred_element_type=F32)

    def residual_and_norm(r):
        x2 = x1_ref[0, r, :] + gt2 * (ya[r] + yb[r])
        x2_scr[r, :] = x2
        return (_rms(x2) * sc3p + sh3).astype(BF16)

    h = _by_rows(residual_and_norm, tm, VEC_ROWS)
    ffn2 = _ffn(h, w2i_ref, w2o_ref)
    for r in range(0, tm, VEC_ROWS):
        x3 = x2_scr[r:r + VEC_ROWS, :] + gt3h * ffn2[r:r + VEC_ROWS]
        o_ref[0, r:r + VEC_ROWS, :] = _rms(x3) * gain


def _post(x1, ya, yb, mod, wo, w2i, w2o, gf):
    bsz, seq, _ = x1.shape
    tm = ROW_TILE
    tok = lambda width: pl.BlockSpec((1, tm, width), lambda b, j: (b, j, 0))
    consts = (wo, w2i, w2o, gf)
    return pl.pallas_call(
        _post_kernel,
        grid=(bsz, seq // tm),
        in_specs=[tok(D_MODEL), tok(FOX_WIDTH), tok(MLSTM_WIDTH),
                  pl.BlockSpec((1, N_MOD, D_MODEL), lambda b, j: (b, 0, 0))]
        + [_resident(a.shape) for a in consts],
        out_specs=tok(D_MODEL),
        out_shape=jax.ShapeDtypeStruct((bsz, seq, D_MODEL), F32),
        scratch_shapes=[pltpu.VMEM((tm, D_MODEL), F32)],
        compiler_params=pltpu.CompilerParams(dimension_semantics=("arbitrary", "arbitrary"),
                                             vmem_limit_bytes=VMEM_LIMIT),
        name="post",
    )(x1, ya, yb, mod, *consts)


def _gate_placement():
    pq = np.zeros((LANES, FOX_HEADS * FOX_GROUP), np.float32)
    pk = np.zeros((LANES, FOX_HEADS * FOX_GROUP), np.float32)
    cq = np.zeros((1, FOX_HEADS * FOX_GROUP), np.float32)
    ck = np.zeros((1, FOX_HEADS * FOX_GROUP), np.float32)
    for h in range(FOX_HEADS):
        base = FOX_GROUP * h + GATE_LANE
        for part in range(3):
            pq[8 * part + h, base + part] = 1.0
            pk[8 * part + h, base + 3 + part] = -1.0
            cq[0, base + 3 + part] = 1.0
            ck[0, base + part] = 1.0
    return jnp.asarray(pq, BF16), jnp.asarray(pk, BF16), jnp.asarray(cq), jnp.asarray(ck)


def kernel(x, c, w_ada, b_ada, w_ffn1_in, w_ffn1_out, w_mix_in, b_fox_f, b_m_i, b_m_f, w_conv, b_conv, g_mhn,
           w_mix_out, w_ffn2_in, w_ffn2_out, g_final):
    assert w_ada.shape[0] == 1, "one layer"
    bsz, seq, _ = x.shape
    assert seq % max(ROW_TILE, MLSTM_SEQ_TILE, FOX_TILE) == 0

    mod, (w1_in, w1_out) = _ada(c, w_ada[0], b_ada[0], (w_ffn1_in[0], w_ffn1_out[0]))
    mod = mod.reshape(bsz, N_MOD, D_MODEL)

    wfox, wml, wg = _split_mix(jnp.swapaxes(w_mix_in, 1, 2)[0])
    zrow = lambda n: jnp.zeros((n,), F32)
    gbias = jnp.concatenate([b_fox_f[0], b_fox_f[0], b_fox_f[0], b_m_i[0], zrow(LANES - 28),
                             zrow(MLSTM_GATE_LANE), b_m_f[0], zrow(LANES - 28)]).reshape(1, 2 * LANES)
    pq, pk, cq, ck = _gate_placement()
    kscale = jnp.concatenate([jnp.ones((MLSTM_WIDTH,), F32),
                              jnp.full((MLSTM_WIDTH,), MLSTM_HEAD_DIM ** -0.5, F32)]).reshape(1, 2 * MLSTM_WIDTH)

    x1, qa, ka, fvt, mqk, mvt, mo, gtok, gt = _pre(
        x, mod, w1_in, w1_out, wfox, wml, wg, gbias,
        pq, pk, cq, ck, w_conv[0], b_conv[0].reshape(1, -1), kscale)

    ya = _fox(qa, ka, fvt)
    yb, (w_out, w2_in, w2_out) = _mlstm(mqk, mvt, mo, gtok, gt, g_mhn[0].reshape(1, MLSTM_WIDTH),
                                        (w_mix_out[0], w_ffn2_in[0], w_ffn2_out[0]))
    return _post(x1, ya, yb, mod, w_out, w2_in, w2_out, g_final.reshape(1, D_MODEL))
```
